```python
import jax, jax.numpy as jnp
from jax import lax
import numpy as np

D_MODEL = 2048
BATCH = 2
SEQ = 4096
DEPTH = 1

GRID_W = 64
CTX_LEN = 256
NORM_EPS = 1e-6

NA_HEADS = 8
NA_HEAD_DIM = 128
NA_WIN_H = 8
NA_WIN_W = 16

GLA_HEADS = 4
GLA_DK = 128
GLA_DV = 256
GLA_GATE_RANK = 16
GLA_GATE_TAU = 16.0
GLA_CHUNK = 64
ROPE_BASE = 10000.0

N_EXPERTS = 64
N_GROUPS = 8
TOPK_GROUPS = 4
TOP_K = 6
D_EXPERT = 512
ROUTED_SCALE = 2.5
MOE_BLOCK = 128

NA_WIDTH = NA_HEADS * NA_HEAD_DIM
GLA_KEY_WIDTH = GLA_HEADS * GLA_DK
GLA_VAL_WIDTH = GLA_HEADS * GLA_DV
MIX_WIDTH = NA_WIDTH + GLA_VAL_WIDTH
SPLIT_SIZES = (NA_WIDTH, NA_WIDTH, NA_WIDTH, GLA_KEY_WIDTH, GLA_KEY_WIDTH,
               GLA_VAL_WIDTH, GLA_VAL_WIDTH, GLA_GATE_RANK, GLA_GATE_RANK)
IN_COLS = sum(SPLIT_SIZES)

kernel_name = 'hybrid_na_gla_moe_dit_layer'


def rms_norm(x, g):
    xf = x.astype(jnp.float32)
    y = xf * lax.rsqrt(jnp.mean(xf * xf, axis=-1, keepdims=True) + NORM_EPS)
    return (y * g.astype(jnp.float32)).astype(x.dtype)


def adaln_params(cond, w_mod, b_mod):
    return jax.nn.silu(cond) @ w_mod + b_mod


def modulate(h, shift, scale):
    return h * (1.0 + scale) + shift


def split_columns(p):
    points, acc = [], 0
    for s in SPLIT_SIZES[:-1]:
        acc += s
        points.append(acc)
    return jnp.split(p, points, axis=-1)


def to_heads(t, n_heads, d_head):
    return t.reshape(t.shape[0], t.shape[1], n_heads, d_head).transpose(0, 2, 1, 3)


def from_heads(t):
    b, n, l, d = t.shape
    return t.transpose(0, 2, 1, 3).reshape(b, l, n * d)


def rope_1d(x, pos):
    half = x.shape[-1] // 2
    inv_freq = ROPE_BASE ** (-jnp.arange(half, dtype=jnp.float32) / half)
    ang = pos.astype(jnp.float32)[:, None] * inv_freq[None, :]
    cos = jnp.cos(ang)[None, :, None, :]
    sin = jnp.sin(ang)[None, :, None, :]
    x1, x2 = x[..., :half], x[..., half:]
    return jnp.concatenate([x1 * cos - x2 * sin, x2 * cos + x1 * sin], axis=-1)


def axial_rope(x, row_pos, col_pos):
    xf = x.astype(jnp.float32)
    half = x.shape[-1] // 2
    return jnp.concatenate([rope_1d(xf[..., :half], row_pos),
                            rope_1d(xf[..., half:], col_pos)], axis=-1).astype(x.dtype)


def neighbourhood_attention(q, k, v, k_ctx, v_ctx, rpb):
    b, h, l, dh = q.shape
    rows = l // GRID_W
    kh = min(NA_WIN_H, rows)
    kw = NA_WIN_W
    scale = dh ** -0.5
    qg = q.reshape(b, h, rows, GRID_W, dh).transpose(2, 0, 1, 3, 4)
    kg = k.reshape(b, h, rows, GRID_W, dh)
    vg = v.reshape(b, h, rows, GRID_W, dh)
    col = jnp.arange(GRID_W)
    col_start = jnp.clip(col - kw // 2, 0, GRID_W - kw)
    col_idx = col_start[:, None] + jnp.arange(kw)[None, :]
    dc = col_idx - col[:, None] + (kw - 1)
    rpb_cols = rpb[:, :, dc]

    def row_block(args):
        r, q_r = args
        rs = jnp.clip(r - kh // 2, 0, rows - kh)
        k_rows = lax.dynamic_slice_in_dim(kg, rs, kh, axis=2)
        v_rows = lax.dynamic_slice_in_dim(vg, rs, kh, axis=2)
        k_win = k_rows[:, :, :, col_idx, :]
        v_win = v_rows[:, :, :, col_idx, :]
        dr = rs + jnp.arange(kh) - r + (NA_WIN_H - 1)
        bias = jnp.take(rpb_cols, dr, axis=1).transpose(0, 2, 1, 3)
        s_loc = jnp.einsum('bhqd,bhiqjd->bhqij', q_r, k_win).astype(jnp.float32) * scale
        s_loc = s_loc + bias.astype(jnp.float32)[None]
        s_ctx = jnp.einsum('bhqd,bhcd->bhqc', q_r, k_ctx).astype(jnp.float32) * scale
        logits = jnp.concatenate([s_loc.reshape(b, h, GRID_W, kh * kw), s_ctx], axis=-1)
        p = jax.nn.softmax(logits, axis=-1).astype(v.dtype)
        p_loc = p[..., :kh * kw].reshape(b, h, GRID_W, kh, kw)
        p_ctx = p[..., kh * kw:]
        return (jnp.einsum('bhqij,bhiqjd->bhqd', p_loc, v_win)
                + jnp.einsum('bhqc,bhcd->bhqd', p_ctx, v_ctx))

    o = lax.map(row_block, (jnp.arange(rows), qg))
    return o.transpose(1, 2, 0, 3, 4).reshape(b, h, l, dh)


def context_attention(q, k, v):
    s = jnp.einsum('bhqd,bhcd->bhqc', q, k).astype(jnp.float32) * (q.shape[-1] ** -0.5)
    p = jax.nn.softmax(s, axis=-1).astype(v.dtype)
    return jnp.einsum('bhqc,bhcd->bhqd', p, v)


def gla_chunk_scan(k, v, g, s0, q=None):
    b, h, l, dk = k.shape
    dv = v.shape[-1]
    n = l // GLA_CHUNK
    mask = jnp.tril(jnp.ones((GLA_CHUNK, GLA_CHUNK), dtype=bool))

    def to_chunks(t):
        return t.reshape(b, h, n, GLA_CHUNK, t.shape[-1]).transpose(2, 0, 1, 3, 4)

    def step(state, xs):
        if q is None:
            kc, vc, gc = xs
            qc = None
        else:
            qc, kc, vc, gc = xs
        cum = jnp.cumsum(gc, axis=2)
        cum_last = cum[:, :, -1:, :]
        new_state = (jnp.exp(cum_last[:, :, 0, :, None]) * state
                     + jnp.einsum('bhsd,bhsv->bhdv', kc * jnp.exp(cum_last - cum), vc))
        if qc is None:
            return new_state, None
        diff = cum[:, :, :, None, :] - cum[:, :, None, :, :]
        decay = jnp.exp(jnp.where(mask[:, :, None], diff, -jnp.inf))
        att = jnp.einsum('bhtd,bhsd,bhtsd->bhts', qc, kc, decay)
        out = (jnp.einsum('bhts,bhsv->bhtv', att, vc)
               + jnp.einsum('bhtd,bhdv->bhtv', qc * jnp.exp(cum), state))
        return new_state, out

    seqs = (k, v, g) if q is None else (q, k, v, g)
    state, out = lax.scan(step, s0, tuple(to_chunks(t) for t in seqs))
    if q is None:
        return None, state
    return out.transpose(1, 2, 0, 3, 4).reshape(b, h, l, dv), state


def gla_bidirectional(q_l, k_l, v_l, gf_l, gb_l, q_c, k_c, v_c, gf_c, gb_c):
    b, h, _, dk = k_l.shape
    dv = v_l.shape[-1]
    s0 = jnp.zeros((b, h, dk, dv), jnp.float32)

    def rev(t):
        return None if t is None else jnp.flip(t, axis=2)

    o_cf, s_cf = gla_chunk_scan(k_c, v_c, gf_c, s0, q_c)
    o_lf, _ = gla_chunk_scan(k_l, v_l, gf_l, s_cf, q_l)
    o_cb, s_cb = gla_chunk_scan(rev(k_c), rev(v_c), rev(gb_c), s0, rev(q_c))
    o_lb, _ = gla_chunk_scan(rev(k_l), rev(v_l), rev(gb_l), s_cb, rev(q_l))
    o_l = o_lf + rev(o_lb)
    o_c = None if q_c is None else o_cf + rev(o_cb)
    return o_l, o_c


def token_mixers(h_lat, h_ctx, w_in, q_norm_g, k_norm_g, na_rpb, gla_gate_up_f, gla_gate_bias_f,
                 gla_gate_up_b, gla_gate_bias_b, gla_norm_g, w_out, need_ctx):
    b, l, _ = h_lat.shape
    p_l = split_columns(h_lat @ w_in)
    p_c = split_columns(h_ctx @ w_in)

    def na_qkv(p):
        bb, ll = p[0].shape[:2]
        q = rms_norm(p[0].reshape(bb, ll, NA_HEADS, NA_HEAD_DIM), q_norm_g)
        k = rms_norm(p[1].reshape(bb, ll, NA_HEADS, NA_HEAD_DIM), k_norm_g)
        v = p[2].reshape(bb, ll, NA_HEADS, NA_HEAD_DIM)
        return q.transpose(0, 2, 1, 3), k.transpose(0, 2, 1, 3), v.transpose(0, 2, 1, 3)

    nq_l, nk_l, nv_l = na_qkv(p_l)
    nq_c, nk_c, nv_c = na_qkv(p_c)
    o_na_l = from_heads(neighbourhood_attention(nq_l, nk_l, nv_l, nk_c, nv_c, na_rpb))

    t = jnp.arange(l)
    row_pos = t // GRID_W
    col_pos = t % GRID_W
    q_scale = GLA_DK ** -0.5

    def gla_qk(p, positioned):
        bb, ll = p[3].shape[:2]
        q = p[3].reshape(bb, ll, GLA_HEADS, GLA_DK)
        k = p[4].reshape(bb, ll, GLA_HEADS, GLA_DK)
        if positioned:
            q = axial_rope(q, row_pos, col_pos)
            k = axial_rope(k, row_pos, col_pos)
        q = (q.astype(jnp.float32) * q_scale).transpose(0, 2, 1, 3)
        k = k.astype(jnp.float32).transpose(0, 2, 1, 3)
        return q, k

    def log_decay(down, up, bias):
        z = (down @ up + bias).astype(jnp.float32)
        return to_heads(jax.nn.log_sigmoid(z) / GLA_GATE_TAU, GLA_HEADS, GLA_DK)

    def gla_values(p):
        return to_heads(p[5].astype(jnp.float32), GLA_HEADS, GLA_DV)

    def gla_out(o, r):
        o = rms_norm(o.transpose(0, 2, 1, 3), gla_norm_g)
        return o.reshape(o.shape[0], o.shape[1], GLA_VAL_WIDTH).astype(r.dtype) * jax.nn.silu(r)

    gq_l, gk_l = gla_qk(p_l, True)
    gq_c, gk_c = gla_qk(p_c, False)
    o_gla_l, o_gla_c = gla_bidirectional(
        gq_l, gk_l, gla_values(p_l),
        log_decay(p_l[7], gla_gate_up_f, gla_gate_bias_f), log_decay(p_l[8], gla_gate_up_b, gla_gate_bias_b),
        gq_c if need_ctx else None, gk_c, gla_values(p_c),
        log_decay(p_c[7], gla_gate_up_f, gla_gate_bias_f), log_decay(p_c[8], gla_gate_up_b, gla_gate_bias_b))

    out_l = jnp.concatenate([o_na_l, gla_out(o_gla_l, p_l[6])], axis=-1) @ w_out
    if not need_ctx:
        return out_l, None
    o_na_c = from_heads(context_attention(nq_c, nk_c, nv_c))
    out_c = jnp.concatenate([o_na_c, gla_out(o_gla_c, p_c[6])], axis=-1) @ w_out
    return out_l, out_c


def moe_ffn(h, router_w, router_bias, w1, w3, w2, ws1, ws3, ws2):
    shape = h.shape
    tok = h.reshape(-1, shape[-1])
    n_tok = tok.shape[0]
    scores = jax.nn.sigmoid((tok @ router_w).astype(jnp.float32))
    biased = scores + router_bias.astype(jnp.float32)
    per_group = N_EXPERTS // N_GROUPS
    group_score = lax.top_k(biased.reshape(n_tok, N_GROUPS, per_group), 2)[0].sum(axis=-1)
    _, gidx = lax.top_k(group_score, TOPK_GROUPS)
    gmask = jnp.sum(jax.nn.one_hot(gidx, N_GROUPS, dtype=jnp.float32), axis=1) > 0
    emask = jnp.repeat(gmask, per_group, axis=1)
    _, eidx = lax.top_k(jnp.where(emask, biased, -jnp.inf), TOP_K)
    sel = jnp.take_along_axis(scores, eidx, axis=1)
    wts = sel / jnp.sum(sel, axis=-1, keepdims=True) * ROUTED_SCALE
    combine = jnp.einsum('tke,tk->te', jax.nn.one_hot(eidx, N_EXPERTS, dtype=jnp.float32), wts).astype(h.dtype)

    def expert_block(args):
        tb, cb = args
        act = jax.nn.silu(jnp.einsum('td,edf->tef', tb, w1)) * jnp.einsum('td,edf->tef', tb, w3)
        return jnp.einsum('tef,efd->td', act * cb[:, :, None], w2)

    n_blk = n_tok // MOE_BLOCK
    routed = lax.map(expert_block, (tok.reshape(n_blk, MOE_BLOCK, shape[-1]),
                                    combine.reshape(n_blk, MOE_BLOCK, N_EXPERTS))).reshape(n_tok, shape[-1])
    shared = (jax.nn.silu(tok @ ws1) * (tok @ ws3)) @ ws2
    return (routed + shared).reshape(shape)


def setup_inputs(seed: int = 0) -> dict:
    key = jax.random.key(seed)
    ks = jax.random.split(key, 26)

    def nrm(k, shape, scale):
        return jax.random.normal(k, shape, jnp.float32) * scale

    return {
        'x': nrm(ks[0], (BATCH, SEQ, D_MODEL), 1.0),
        'c': nrm(ks[1], (BATCH, D_MODEL), 1.0),
        'ctx': nrm(ks[2], (BATCH, CTX_LEN, D_MODEL), 1.0),
        'c_ctx': nrm(ks[3], (D_MODEL,), 1.0),
        'w_mod': nrm(ks[4], (DEPTH, D_MODEL, 6 * D_MODEL), 0.5 * D_MODEL ** -0.5),
        'b_mod': nrm(ks[5], (DEPTH, 6 * D_MODEL), 0.02),
        'norm1_g': 1.0 + nrm(ks[6], (DEPTH, D_MODEL), 0.02),
        'norm2_g': 1.0 + nrm(ks[7], (DEPTH, D_MODEL), 0.02),
        'w_in': nrm(ks[8], (DEPTH, D_MODEL, IN_COLS), D_MODEL ** -0.5),
        'q_norm_g': 1.0 + nrm(ks[9], (DEPTH, NA_HEAD_DIM), 0.02),
        'k_norm_g': 1.0 + nrm(ks[10], (DEPTH, NA_HEAD_DIM), 0.02),
        'na_rpb': nrm(ks[11], (DEPTH, NA_HEADS, 2 * NA_WIN_H - 1, 2 * NA_WIN_W - 1), 0.1),
        'gla_gate_up_f': nrm(ks[12], (DEPTH, GLA_GATE_RANK, GLA_KEY_WIDTH), GLA_GATE_RANK ** -0.5),
        'gla_gate_bias_f': nrm(ks[13], (DEPTH, GLA_KEY_WIDTH), 0.1),
        'gla_gate_up_b': nrm(ks[14], (DEPTH, GLA_GATE_RANK, GLA_KEY_WIDTH), GLA_GATE_RANK ** -0.5),
        'gla_gate_bias_b': nrm(ks[15], (DEPTH, GLA_KEY_WIDTH), 0.1),
        'gla_norm_g': 1.0 + nrm(ks[16], (DEPTH, GLA_DV), 0.02),
        'w_out': nrm(ks[17], (DEPTH, MIX_WIDTH, D_MODEL), MIX_WIDTH ** -0.5),
        'router_w': nrm(ks[18], (DEPTH, D_MODEL, N_EXPERTS), D_MODEL ** -0.5),
        'router_bias': nrm(ks[19], (DEPTH, N_EXPERTS), 0.01),
        'expert_w1': nrm(ks[20], (DEPTH, N_EXPERTS, D_MODEL, D_EXPERT), D_MODEL ** -0.5),
        'expert_w3': nrm(ks[21], (DEPTH, N_EXPERTS, D_MODEL, D_EXPERT), D_MODEL ** -0.5),
        'expert_w2': nrm(ks[22], (DEPTH, N_EXPERTS, D_EXPERT, D_MODEL), D_EXPERT ** -0.5),
        'shared_w1': nrm(ks[23], (DEPTH, D_MODEL, D_EXPERT), D_MODEL ** -0.5),
        'shared_w3': nrm(ks[24], (DEPTH, D_MODEL, D_EXPERT), D_MODEL ** -0.5),
        'shared_w2': nrm(ks[25], (DEPTH, D_EXPERT, D_MODEL), D_EXPERT ** -0.5),
    }


def reference(x, c, ctx, c_ctx, w_mod, b_mod, norm1_g, norm2_g, w_in, q_norm_g, k_norm_g, na_rpb,
              gla_gate_up_f, gla_gate_bias_f, gla_gate_up_b, gla_gate_bias_b, gla_norm_g, w_out,
              router_w, router_bias, expert_w1, expert_w3, expert_w2, shared_w1, shared_w3, shared_w2):
    for layer in range(DEPTH):
        need_ctx = layer < DEPTH - 1
        sh1, sc1, g1, sh2, sc2, g2 = jnp.split(
            adaln_params(c, w_mod[layer], b_mod[layer])[:, None, :], 6, axis=-1)
        csh1, csc1, cg1, csh2, csc2, cg2 = jnp.split(
            adaln_params(c_ctx, w_mod[layer], b_mod[layer]), 6, axis=-1)
        h_l = modulate(rms_norm(x, norm1_g[layer]), sh1, sc1)
        h_c = modulate(rms_norm(ctx, norm1_g[layer]), csh1, csc1)
        o_l, o_c = token_mixers(h_l, h_c, w_in[layer], q_norm_g[layer], k_norm_g[layer], na_rpb[layer],
                                gla_gate_up_f[layer], gla_gate_bias_f[layer], gla_gate_up_b[layer],
                                gla_gate_bias_b[layer], gla_norm_g[layer], w_out[layer], need_ctx)
        x = x + g1 * o_l
        x = x + g2 * moe_ffn(modulate(rms_norm(x, norm2_g[layer]), sh2, sc2), router_w[layer],
                             router_bias[layer], expert_w1[layer], expert_w3[layer], expert_w2[layer],
                             shared_w1[layer], shared_w3[layer], shared_w2[layer])
        if need_ctx:
            ctx = ctx + cg1 * o_c
            ctx = ctx + cg2 * moe_ffn(modulate(rms_norm(ctx, norm2_g[layer]), csh2, csc2), router_w[layer],
                                      router_bias[layer], expert_w1[layer], expert_w3[layer],
                                      expert_w2[layer], shared_w1[layer], shared_w3[layer],
                                      shared_w2[layer])
    return x
```

```python
import functools

import jax
import jax.numpy as jnp
from jax import lax
from jax.experimental import pallas as pl
from jax.experimental.pallas import tpu as pltpu

f32 = jnp.float32
bf16 = jnp.bfloat16
i32 = jnp.int32
u32 = jnp.uint32

GRID_W = 64
NORM_EPS = 1e-6
NA_HEADS = 8
NA_HEAD_DIM = 128
NA_WIN_H = 8
NA_WIN_W = 16
GLA_HEADS = 4
GLA_DK = 128
GLA_DV = 256
GLA_GATE_RANK = 16
GLA_GATE_TAU = 16.0
GLA_CHUNK = 64
GLA_SUB = 16
ROPE_BASE = 10000.0
N_EXPERTS = 64
N_GROUPS = 8
TOPK_GROUPS = 4
TOP_K = 6
D_EXPERT = 512
ROUTED_SCALE = 2.5

NA_WIDTH = NA_HEADS * NA_HEAD_DIM
GLA_KEY_WIDTH = GLA_HEADS * GLA_DK
GLA_VAL_WIDTH = GLA_HEADS * GLA_DV
COL_NQ, COL_NK, COL_NV = 0, NA_WIDTH, 2 * NA_WIDTH
COL_GQ = 3 * NA_WIDTH
COL_GK = COL_GQ + GLA_KEY_WIDTH
COL_GV = COL_GK + GLA_KEY_WIDTH
COL_GATE = COL_GV + GLA_VAL_WIDTH
COL_DD = COL_GATE + GLA_VAL_WIDTH
PROJ_TN = 512
NA_QROWS = 4
NA_KROWS = 12
MOE_TM = 256
NEG = -1e30

VMEM_LIMIT = 56 * 1024 * 1024


def _cparams(sem, vmem=VMEM_LIMIT):
    return pltpu.CompilerParams(dimension_semantics=sem, vmem_limit_bytes=vmem)


NT_DIMS = (((1,), (1,)), ((), ()))
TN_DIMS = (((0,), (0,)), ((), ()))


def _pack_pair(lo, hi):
    lo_b = lax.bitcast_convert_type(lo.astype(bf16).astype(f32), u32)
    hi_b = lax.bitcast_convert_type(hi.astype(bf16).astype(f32), u32)
    return (lo_b >> 16) | (hi_b & jnp.uint32(0xFFFF0000))


def _unpack_pair(p):
    lo = lax.bitcast_convert_type(p << 16, f32)
    hi = lax.bitcast_convert_type(p & jnp.uint32(0xFFFF0000), f32)
    return lo, hi


def _mod_kernel(c_ref, w_ref, b_ref, o_ref):
    c = c_ref[...]
    s = (c * jax.nn.sigmoid(c)).astype(bf16)
    o_ref[...] = jnp.dot(s, w_ref[...].astype(bf16), preferred_element_type=f32) + b_ref[...]


def _adaln(cond8, w_mod, b_mod):
    d, n = w_mod.shape
    tn = 1024
    return pl.pallas_call(
        _mod_kernel,
        out_shape=jax.ShapeDtypeStruct((8, n), f32),
        grid=(n // tn,),
        in_specs=[pl.BlockSpec((8, d), lambda j: (0, 0)),
                  pl.BlockSpec((d, tn), lambda j: (0, j)),
                  pl.BlockSpec((1, tn), lambda j: (0, j))],
        out_specs=pl.BlockSpec((8, tn), lambda j: (0, j)),
        compiler_params=_cparams(("arbitrary",)),
        name="mod",
    )(cond8, w_mod, b_mod.reshape(1, n))


def _head_rms(a, g, width):
    parts = []
    for h in range(a.shape[-1] // width):
        ah = a[:, h * width:(h + 1) * width]
        ms = jnp.mean(ah * ah, axis=-1, keepdims=True)
        parts.append(ah * lax.rsqrt(ms + NORM_EPS) * g)
    return jnp.concatenate(parts, axis=-1)


def _rope(a, cos, sin_lo, sin_hi):
    parts = []
    for h in range(a.shape[-1] // GLA_DK):
        ah = a[:, h * GLA_DK:(h + 1) * GLA_DK]
        parts.append(ah * cos + pltpu.roll(ah, 32, axis=1) * sin_hi + pltpu.roll(ah, 96, axis=1) * sin_lo)
    return jnp.concatenate(parts, axis=-1)


def _inproj_kernel(x_ref, sc_ref, sh_ref, g_ref, w_ref, wdd_ref, qg_ref, kg_ref, cos_ref, slo_ref, shi_ref,
                   o_ref, dd_ref, h_scr, *, rope):
    j = pl.program_id(1)

    @pl.when(j == 0)
    def _():
        x = x_ref[...]
        ms = jnp.mean(x * x, axis=-1, keepdims=True)
        y = x * lax.rsqrt(ms + NORM_EPS) * g_ref[...]
        hb = (y * (1.0 + sc_ref[0]) + sh_ref[0]).astype(bf16)
        h_scr[...] = hb
        dd_ref[...] = jnp.dot(hb, wdd_ref[...], preferred_element_type=f32)

    acc = jnp.dot(h_scr[...], w_ref[...], preferred_element_type=f32)
    jq, jk, jgq, jgk, jgate = (COL_NQ // PROJ_TN, COL_NK // PROJ_TN, COL_GQ // PROJ_TN, COL_GK // PROJ_TN,
                               COL_GATE // PROJ_TN)

    @pl.when(j < jk)
    def _():
        o_ref[...] = _head_rms(acc, qg_ref[...], NA_HEAD_DIM).astype(bf16)

    @pl.when((j >= jk) & (j < COL_NV // PROJ_TN))
    def _():
        o_ref[...] = _head_rms(acc, kg_ref[...], NA_HEAD_DIM).astype(bf16)

    @pl.when(j == jgq)
    def _():
        a = _rope(acc, cos_ref[...], slo_ref[...], shi_ref[...]) if rope else acc
        o_ref[...] = (a * (GLA_DK ** -0.5)).astype(bf16)

    @pl.when(j == jgk)
    def _():
        a = _rope(acc, cos_ref[...], slo_ref[...], shi_ref[...]) if rope else acc
        o_ref[...] = a.astype(bf16)

    @pl.when(j >= jgate)
    def _():
        o_ref[...] = (acc * jax.nn.sigmoid(acc)).astype(bf16)

    @pl.when(((j >= COL_NV // PROJ_TN) & (j < jgq)) | ((j > jgk) & (j < jgate)))
    def _():
        o_ref[...] = acc.astype(bf16)


def _inproj(x2d, sc, sh, norm_g, w_bf, wdd_bf, qg, kg, rope_tabs, *, tm, tiles_per_group, rope):
    m, d = x2d.shape
    n = w_bf.shape[1]
    cos, slo, shi = rope_tabs
    rope_tiles = cos.shape[0] // tm
    grid = (m // tm, n // PROJ_TN)
    vec = lambda i, j: (0, 0)
    return pl.pallas_call(
        functools.partial(_inproj_kernel, rope=rope),
        out_shape=[jax.ShapeDtypeStruct((m, n), bf16), jax.ShapeDtypeStruct((m, 128), f32)],
        grid=grid,
        in_specs=[pl.BlockSpec((tm, d), lambda i, j: (i, 0)),
                  pl.BlockSpec((1, 1, d), lambda i, j: (i // tiles_per_group, 0, 0)),
                  pl.BlockSpec((1, 1, d), lambda i, j: (i // tiles_per_group, 0, 0)),
                  pl.BlockSpec((1, d), vec),
                  pl.BlockSpec((d, PROJ_TN), lambda i, j: (0, j)),
                  pl.BlockSpec((d, 128), vec),
                  pl.BlockSpec((1, NA_HEAD_DIM), vec),
                  pl.BlockSpec((1, NA_HEAD_DIM), vec),
                  pl.BlockSpec((tm, GLA_DK), lambda i, j: (i % rope_tiles, 0)),
                  pl.BlockSpec((tm, GLA_DK), lambda i, j: (i % rope_tiles, 0)),
                  pl.BlockSpec((tm, GLA_DK), lambda i, j: (i % rope_tiles, 0))],
        out_specs=[pl.BlockSpec((tm, PROJ_TN), lambda i, j: (i, j)),
                   pl.BlockSpec((tm, 128), lambda i, j: (i, 0))],
        scratch_shapes=[pltpu.VMEM((tm, d), bf16)],
        compiler_params=_cparams(("arbitrary", "arbitrary")),
        name="inproj_rope" if rope else "inproj_ctx",
    )(x2d, sc, sh, norm_g, w_bf, wdd_bf, qg, kg, cos, slo, shi)


def _rope_tables(l):
    t = jnp.arange(l)
    half = GLA_DK // 4
    inv_freq = ROPE_BASE ** (-jnp.arange(half, dtype=f32) / half)
    ang_r = (t // GRID_W).astype(f32)[:, None] * inv_freq[None, :]
    ang_c = (t % GRID_W).astype(f32)[:, None] * inv_freq[None, :]
    cos = jnp.concatenate([jnp.cos(ang_r)] * 2 + [jnp.cos(ang_c)] * 2, axis=-1)
    sin = jnp.concatenate([jnp.sin(ang_r)] * 2 + [jnp.sin(ang_c)] * 2, axis=-1)
    lane = jnp.arange(GLA_DK) % (2 * half)
    sin_lo = jnp.where(lane[None, :] < half, -sin, 0.0)
    sin_hi = jnp.where(lane[None, :] >= half, sin, 0.0)
    return cos, sin_lo, sin_hi


def _na_kernel(q_ref, k_ref, v_ref, kc_ref, vc_ref, bias_ref, o_ref, *, rows):
    a = pl.program_id(2)
    start = pl.multiple_of(jnp.clip(NA_QROWS * a - NA_WIN_H // 2, 0, rows - NA_KROWS) * GRID_W, GRID_W)
    scale = NA_HEAD_DIM ** -0.5
    q = q_ref[...]
    k = k_ref[pl.ds(start, NA_KROWS * GRID_W), :]
    v = v_ref[pl.ds(start, NA_KROWS * GRID_W), :]
    s_loc = lax.dot_general(q, k, NT_DIMS, preferred_element_type=f32) * scale + bias_ref[0, 0]
    s_ctx = lax.dot_general(q, kc_ref[...], NT_DIMS, preferred_element_type=f32) * scale
    m = jnp.maximum(jnp.max(s_loc, axis=-1, keepdims=True), jnp.max(s_ctx, axis=-1, keepdims=True))
    p_loc = jnp.exp(s_loc - m)
    p_ctx = jnp.exp(s_ctx - m)
    denom = jnp.sum(p_loc, axis=-1, keepdims=True) + jnp.sum(p_ctx, axis=-1, keepdims=True)
    o = (jnp.dot(p_loc.astype(bf16), v, preferred_element_type=f32)
         + jnp.dot(p_ctx.astype(bf16), vc_ref[...], preferred_element_type=f32))
    o_ref[...] = (o / denom).astype(bf16)


def _na_bias_tables(rpb, rows):
    n_a = rows // NA_QROWS
    qr = jnp.arange(NA_QROWS)[:, None]
    kr = jnp.arange(NA_KROWS)[None, :]
    c = jnp.arange(GRID_W)[:, None]
    kc = jnp.arange(GRID_W)[None, :]
    cs = jnp.clip(c - NA_WIN_W // 2, 0, GRID_W - NA_WIN_W)
    col_ok = (kc >= cs) & (kc < cs + NA_WIN_W)
    dc = jnp.clip(kc - c + NA_WIN_W - 1, 0, 2 * NA_WIN_W - 2)
    tabs = []
    for a in (0, 1, n_a - 1):
        start = min(max(NA_QROWS * a - NA_WIN_H // 2, 0), rows - NA_KROWS)
        r_abs = NA_QROWS * a + qr
        k_abs = start + kr
        rs = jnp.clip(r_abs - NA_WIN_H // 2, 0, rows - NA_WIN_H)
        row_ok = (k_abs >= rs) & (k_abs < rs + NA_WIN_H)
        dr = jnp.clip(k_abs - r_abs + NA_WIN_H - 1, 0, 2 * NA_WIN_H - 2)
        b = rpb[:, dr[:, None, :, None], dc[None, :, None, :]]
        ok = row_ok[:, None, :, None] & col_ok[None, :, None, :]
        tabs.append(jnp.where(ok[None], b, NEG).reshape(rpb.shape[0], NA_QROWS * GRID_W, NA_KROWS * GRID_W))
    return jnp.stack(tabs, axis=1).astype(f32)


def _na(p_lat, p_ctx, bias, batch, l, lc):
    rows = l // GRID_W
    n_a = rows // NA_QROWS
    tq = NA_QROWS * GRID_W
    hk, hv = COL_NK // NA_HEAD_DIM, COL_NV // NA_HEAD_DIM

    def variant(a):
        return jnp.where(a == 0, 0, jnp.where(a == n_a - 1, 2, 1))

    return pl.pallas_call(
        functools.partial(_na_kernel, rows=rows),
        out_shape=jax.ShapeDtypeStruct((batch * l, NA_WIDTH), bf16),
        grid=(batch, NA_HEADS, n_a),
        in_specs=[pl.BlockSpec((tq, NA_HEAD_DIM), lambda b, h, a: (b * n_a + a, h)),
                  pl.BlockSpec((l, NA_HEAD_DIM), lambda b, h, a: (b, hk + h)),
                  pl.BlockSpec((l, NA_HEAD_DIM), lambda b, h, a: (b, hv + h)),
                  pl.BlockSpec((lc, NA_HEAD_DIM), lambda b, h, a: (b, hk + h)),
                  pl.BlockSpec((lc, NA_HEAD_DIM), lambda b, h, a: (b, hv + h)),
                  pl.BlockSpec((1, 1, tq, NA_KROWS * GRID_W), lambda b, h, a: (h, variant(a), 0, 0))],
        out_specs=pl.BlockSpec((tq, NA_HEAD_DIM), lambda b, h, a: (b * n_a + a, h)),
        compiler_params=_cparams(("arbitrary", "arbitrary", "arbitrary")),
        name="na",
    )(p_lat, p_lat, p_lat, p_ctx, p_ctx, bias)


def _log_decay(dd, up, bias):
    z = jnp.dot(dd.astype(bf16), up, preferred_element_type=f32) + bias
    return (jnp.minimum(z, 0.0) - jnp.log1p(jnp.exp(-jnp.abs(z)))) * (1.0 / GLA_GATE_TAU)


def _gla_chunk(q, k, v, g, st, *, rev, need_out):
    c, sub = GLA_CHUNK, GLA_SUB
    r = lax.broadcasted_iota(i32, (c, c), 0)
    col = lax.broadcasted_iota(i32, (c, c), 1)
    tri = ((r <= col) if rev else (r >= col)).astype(f32)
    cum = jnp.dot(tri, g, precision=lax.Precision.HIGHEST, preferred_element_type=f32)
    total = cum[0:1] if rev else cum[c - 1:c]
    kd = (k * jnp.exp(total - cum)).astype(bf16)
    st_new = st * jnp.exp(total) + lax.dot_general(v, kd, TN_DIMS, preferred_element_type=f32)
    if not need_out:
        return None, st_new
    qd = (q * jnp.exp(cum)).astype(bf16)
    o_inter = lax.dot_general(qd, st.astype(bf16), NT_DIMS, preferred_element_type=f32)
    t_io = lax.broadcasted_iota(i32, (sub, 1), 0)
    outs = []
    for i in range(c // sub):
        lo, hi = i * sub, (i + 1) * sub
        q_i, k_i, cum_i, v_i = q[lo:hi], k[lo:hi], cum[lo:hi], v[lo:hi].astype(f32)
        o_i = jnp.zeros((sub, GLA_DV), f32)
        klo, khi = (hi, c) if rev else (0, lo)
        if khi > klo:
            edge = cum[hi:hi + 1] if rev else cum[lo - 1:lo]
            qe = (q_i * jnp.exp(cum_i - edge)).astype(bf16)
            ke = (k[klo:khi] * jnp.exp(edge - cum[klo:khi])).astype(bf16)
            att = lax.dot_general(qe, ke, NT_DIMS, preferred_element_type=f32)
            o_i = o_i + jnp.dot(att.astype(bf16), v[klo:khi], preferred_element_type=f32)
        for s in range(sub):
            ok = (t_io <= s) if rev else (t_io >= s)
            e = jnp.exp(jnp.where(ok, cum_i - cum_i[s:s + 1], NEG))
            a_col = jnp.sum(q_i * k_i[s:s + 1] * e, axis=-1, keepdims=True)
            o_i = o_i + a_col * v_i[s:s + 1]
        outs.append(o_i)
    return o_inter + jnp.concatenate(outs, axis=0), st_new


def _gla_kernel(qf_ref, kf_ref, vf_ref, ddf_ref, qb_ref, kb_ref, vb_ref, ddb_ref, kc_ref, vc_ref, ddc_ref,
                upf_ref, upb_ref, bf_ref, bb_ref, of_ref, ob_ref, sf_scr, sb_scr):
    j = pl.program_id(2)
    n_chunks = kc_ref.shape[0] // GLA_CHUNK
    upf, upb, bias_f, bias_b = upf_ref[0], upb_ref[0], bf_ref[...], bb_ref[...]

    def rows(ref, ci):
        return ref[ci * GLA_CHUNK:(ci + 1) * GLA_CHUNK, :]

    @pl.when(j == 0)
    def _():
        sf = jnp.zeros((GLA_DV, GLA_DK), f32)
        sb = jnp.zeros((GLA_DV, GLA_DK), f32)
        for ci in range(n_chunks):
            cb = n_chunks - 1 - ci
            gf = _log_decay(rows(ddc_ref, ci), upf, bias_f)
            gb = _log_decay(rows(ddc_ref, cb), upb, bias_b)
            _, sf = _gla_chunk(None, rows(kc_ref, ci).astype(f32), rows(vc_ref, ci), gf, sf, rev=False,
                               need_out=False)
            _, sb = _gla_chunk(None, rows(kc_ref, cb).astype(f32), rows(vc_ref, cb), gb, sb, rev=True,
                               need_out=False)
        sf_scr[...] = sf
        sb_scr[...] = sb

    @pl.when(j > 0)
    def _():
        sf = sf_scr[...]
        sb = sb_scr[...]
        nc = kf_ref.shape[0] // GLA_CHUNK
        for ci in range(nc):
            cb = nc - 1 - ci
            gf = _log_decay(rows(ddf_ref, ci), upf, bias_f)
            gb = _log_decay(rows(ddb_ref, cb), upb, bias_b)
            o_f, sf = _gla_chunk(rows(qf_ref, ci).astype(f32), rows(kf_ref, ci).astype(f32), rows(vf_ref, ci), gf,
                                 sf, rev=False, need_out=True)
            o_b, sb = _gla_chunk(rows(qb_ref, cb).astype(f32), rows(kb_ref, cb).astype(f32), rows(vb_ref, cb), gb,
                                 sb, rev=True, need_out=True)
            of_ref[ci * GLA_CHUNK:(ci + 1) * GLA_CHUNK, :] = o_f.astype(bf16)
            ob_ref[cb * GLA_CHUNK:(cb + 1) * GLA_CHUNK, :] = o_b.astype(bf16)
        sf_scr[...] = sf
        sb_scr[...] = sb


def _gla(p_lat, dd_lat, p_ctx, dd_ctx, upf, upb, bias_f, bias_b, batch, l, lc):
    tt = lc
    n_l = l // tt
    hq, hk, hv = COL_GQ // GLA_DK, COL_GK // GLA_DK, COL_GV // GLA_DV

    def fwd(b, h, j):
        return b * n_l + jnp.maximum(j - 1, 0)

    def bwd(b, h, j):
        return b * n_l + n_l - jnp.maximum(j, 1)

    lat = lambda rowmap, width, col: pl.BlockSpec((tt, width), lambda b, h, j: (rowmap(b, h, j), col + h))
    ddspec = lambda rowmap: pl.BlockSpec((tt, 128), lambda b, h, j: (rowmap(b, h, j), 0))
    ctx = lambda width, col: pl.BlockSpec((tt, width), lambda b, h, j: (b, col + h))
    return pl.pallas_call(
        _gla_kernel,
        out_shape=[jax.ShapeDtypeStruct((batch * l, GLA_VAL_WIDTH), bf16)] * 2,
        grid=(batch, GLA_HEADS, n_l + 1),
        in_specs=[lat(fwd, GLA_DK, hq), lat(fwd, GLA_DK, hk), lat(fwd, GLA_DV, hv), ddspec(fwd),
                  lat(bwd, GLA_DK, hq), lat(bwd, GLA_DK, hk), lat(bwd, GLA_DV, hv), ddspec(bwd),
                  ctx(GLA_DK, hk), ctx(GLA_DV, hv), pl.BlockSpec((tt, 128), lambda b, h, j: (b, 0)),
                  pl.BlockSpec((1, 128, GLA_DK), lambda b, h, j: (h, 0, 0)),
                  pl.BlockSpec((1, 128, GLA_DK), lambda b, h, j: (h, 0, 0)),
                  pl.BlockSpec((1, GLA_DK), lambda b, h, j: (0, h)),
                  pl.BlockSpec((1, GLA_DK), lambda b, h, j: (0, h))],
        out_specs=[pl.BlockSpec((tt, GLA_DV), lambda b, h, j: (fwd(b, h, j), h)),
                   pl.BlockSpec((tt, GLA_DV), lambda b, h, j: (bwd(b, h, j), h))],
        scratch_shapes=[pltpu.VMEM((GLA_DV, GLA_DK), f32), pltpu.VMEM((GLA_DV, GLA_DK), f32)],
        compiler_params=_cparams(("arbitrary", "arbitrary", "arbitrary")),
        name="gla",
    )(p_lat, p_lat, p_lat, dd_lat, p_lat, p_lat, p_lat, dd_lat, p_ctx, p_ctx, dd_ctx, upf, upb, bias_f, bias_b)


def _pad_gate_up(up, row0):
    w = up.reshape(GLA_GATE_RANK, GLA_HEADS, GLA_DK).transpose(1, 0, 2)
    return jnp.zeros((GLA_HEADS, 128, GLA_DK), f32).at[:, row0:row0 + GLA_GATE_RANK].set(w).astype(bf16)


def _outproj_kernel(ona_ref, of_ref, ob_ref, gate_ref, gng_ref, w_ref, x_ref, g1_ref, n2g_ref, sc2_ref, sh2_ref,
                    rwt_ref, x1_ref, h2p_ref, lg_ref):
    o = of_ref[...].astype(f32) + ob_ref[...].astype(f32)
    gla = (_head_rms(o, gng_ref[...], GLA_DV) * gate_ref[...].astype(f32)).astype(bf16)
    acc = (jnp.dot(ona_ref[...], w_ref[0:NA_WIDTH, :], preferred_element_type=f32)
           + jnp.dot(gla, w_ref[NA_WIDTH:, :], preferred_element_type=f32))
    x1 = x_ref[...] + g1_ref[0] * acc
    x1_ref[...] = x1
    ms = jnp.mean(x1 * x1, axis=-1, keepdims=True)
    h2 = x1 * lax.rsqrt(ms + NORM_EPS) * n2g_ref[...] * (1.0 + sc2_ref[0]) + sh2_ref[0]
    lg_ref[...] = lax.dot_general(rwt_ref[...], h2, NT_DIMS, precision=lax.Precision.HIGHEST,
                                  preferred_element_type=f32)
    half = h2.shape[1] // 2
    h2p_ref[...] = _pack_pair(h2[:, :half], h2[:, half:])


def _outproj(o_na, o_f, o_b, p_lat, gnorm_g, w_out_bf, x2d, g1, norm2_g, sc2, sh2, rwt, tiles_per_batch, tm):
    m, d = x2d.shape
    vec = lambda i: (0, 0)
    per_b = lambda i: (i // tiles_per_batch, 0, 0)
    return pl.pallas_call(
        _outproj_kernel,
        out_shape=[jax.ShapeDtypeStruct((m, d), f32), jax.ShapeDtypeStruct((m, d // 2), u32),
                   jax.ShapeDtypeStruct((N_EXPERTS, m), f32)],
        grid=(m // tm,),
        in_specs=[pl.BlockSpec((tm, NA_WIDTH), lambda i: (i, 0)),
                  pl.BlockSpec((tm, GLA_VAL_WIDTH), lambda i: (i, 0)),
                  pl.BlockSpec((tm, GLA_VAL_WIDTH), lambda i: (i, 0)),
                  pl.BlockSpec((tm, GLA_VAL_WIDTH), lambda i: (i, COL_GATE // GLA_VAL_WIDTH)),
                  pl.BlockSpec((1, GLA_DV), vec),
                  pl.BlockSpec(w_out_bf.shape, vec),
                  pl.BlockSpec((tm, d), lambda i: (i, 0)),
                  pl.BlockSpec((1, 1, d), per_b),
                  pl.BlockSpec((1, d), vec),
                  pl.BlockSpec((1, 1, d), per_b),
                  pl.BlockSpec((1, 1, d), per_b),
                  pl.BlockSpec((N_EXPERTS, d), vec)],
        out_specs=[pl.BlockSpec((tm, d), lambda i: (i, 0)),
                   pl.BlockSpec((tm, d // 2), lambda i: (i, 0)),
                   pl.BlockSpec((N_EXPERTS, tm), lambda i: (0, i))],
        compiler_params=_cparams(("arbitrary",)),
        name="outproj",
    )(o_na, o_f, o_b, p_lat, gnorm_g, w_out_bf, x2d, g1, norm2_g, sc2, sh2, rwt)


def _first_argmax(vals, iota, axis, sentinel):
    mx = jnp.max(vals, axis=axis, keepdims=True)
    am = jnp.min(jnp.where(vals == mx, iota, sentinel), axis=axis, keepdims=True)
    return mx, am


def _route_tile(lg, rb):
    t = lg.shape[1]
    per = N_EXPERTS // N_GROUPS
    scores = jax.nn.sigmoid(lg)
    biased = scores + rb
    b3 = biased.reshape(N_GROUPS, per, t)
    io3 = lax.broadcasted_iota(i32, (N_GROUPS, per, t), 1)
    m1, a1 = _first_argmax(b3, io3, 1, per)
    m2 = jnp.max(jnp.where(io3 == a1, -jnp.inf, b3), axis=1, keepdims=True)
    gs = (m1 + m2).reshape(N_GROUPS, t)
    iog = lax.broadcasted_iota(i32, (N_GROUPS, t), 0)
    gsel = jnp.zeros((N_GROUPS, t), jnp.bool_)
    for _ in range(TOPK_GROUPS):
        _, am = _first_argmax(gs, iog, 0, N_GROUPS)
        hit = iog == am
        gsel = gsel | hit
        gs = jnp.where(hit, -jnp.inf, gs)
    emask = jnp.broadcast_to(gsel[:, None, :], (N_GROUPS, per, t)).reshape(N_EXPERTS, t)
    masked = jnp.where(emask, biased, -jnp.inf)
    ioe = lax.broadcasted_iota(i32, (N_EXPERTS, t), 0)
    hits, sels = [], []
    for _ in range(TOP_K):
        _, am = _first_argmax(masked, ioe, 0, N_EXPERTS)
        hit = ioe == am
        hits.append(hit)
        sels.append(jnp.sum(jnp.where(hit, scores, 0.0), axis=0, keepdims=True))
        masked = jnp.where(hit, -jnp.inf, masked)
    denom = sels[0]
    for s in sels[1:]:
        denom = denom + s
    wts = [s / denom * ROUTED_SCALE for s in sels]
    return hits, wts


def _route_kernel(lg_ref, rb_ref, pos_ref, wts_ref, meta_ref, blk_ref, cnt_scr, carry_scr, base_scr, tri_scr):
    ph = pl.program_id(0)
    i = pl.program_id(1)
    t = lg_ref.shape[1]
    hits, wts = _route_tile(lg_ref[...], rb_ref[...])
    sel = hits[0]
    for h in hits[1:]:
        sel = sel | h
    self32 = sel.astype(f32)
    tile_cnt = jnp.sum(self32, axis=1, keepdims=True)

    @pl.when((ph == 0) & (i == 0))
    def _():
        cnt_scr[...] = jnp.zeros_like(cnt_scr)

    @pl.when(ph == 0)
    def _():
        cnt_scr[...] += jnp.broadcast_to(tile_cnt, cnt_scr.shape)

    @pl.when((ph == 1) & (i == 0))
    def _():
        cnt = cnt_scr[...]
        nblk = jnp.floor((cnt + (MOE_TM - 1)) * (1.0 / MOE_TM))
        r = lax.broadcasted_iota(i32, (N_EXPERTS, N_EXPERTS), 0)
        c = lax.broadcasted_iota(i32, (N_EXPERTS, N_EXPERTS), 1)
        excl = jnp.dot((c < r).astype(bf16), nblk.astype(bf16), preferred_element_type=f32)
        base_scr[...] = excl * MOE_TM
        carry_scr[...] = jnp.zeros_like(carry_scr)
        meta_ref[0] = cnt.astype(i32)
        meta_ref[1] = (excl * MOE_TM).astype(i32)
        incl = excl + nblk
        nb = blk_ref.shape[1]
        jj = lax.broadcasted_iota(i32, (N_EXPERTS, nb), 1).astype(f32)
        owner = jnp.sum((jnp.broadcast_to(incl[:, 0:1], (N_EXPERTS, nb)) <= jj).astype(i32), axis=0, keepdims=True)
        blk_ref[0:1, :] = jnp.minimum(owner, N_EXPERTS - 1)
        blk_ref[1:2, :] = jnp.broadcast_to(incl[N_EXPERTS - 1:N_EXPERTS, 0:1].astype(i32), (1, nb))
        blk_ref[2:8, :] = jnp.zeros((6, nb), i32)
        rr = lax.broadcasted_iota(i32, (t, t), 0)
        cc = lax.broadcasted_iota(i32, (t, t), 1)
        tri_scr[...] = (rr < cc).astype(bf16)

    @pl.when(ph == 1)
    def _():
        rank = jnp.dot(sel.astype(bf16), tri_scr[...], preferred_element_type=f32)
        posf = base_scr[:, 0:1] + carry_scr[:, 0:1] + rank
        for k in range(TOP_K):
            pos_ref[k:k + 1, :] = jnp.sum(jnp.where(hits[k], posf, 0.0), axis=0, keepdims=True).astype(i32)
            wts_ref[k:k + 1, :] = wts[k]
        pos_ref[TOP_K:8, :] = jnp.zeros((8 - TOP_K, t), i32)
        wts_ref[TOP_K:8, :] = jnp.zeros((8 - TOP_K, t), f32)
        carry_scr[...] += jnp.broadcast_to(tile_cnt, carry_scr.shape)


def _route(logits_t, router_bias, n_blocks):
    n_tok = logits_t.shape[1]
    tt = 1024
    n_tiles = n_tok // tt
    return pl.pallas_call(
        _route_kernel,
        out_shape=[jax.ShapeDtypeStruct((8, n_tok), i32), jax.ShapeDtypeStruct((8, n_tok), f32),
                   jax.ShapeDtypeStruct((2, N_EXPERTS, 128), i32), jax.ShapeDtypeStruct((8, n_blocks), i32)],
        grid=(2, n_tiles),
        in_specs=[pl.BlockSpec((N_EXPERTS, tt), lambda p, i: (0, i)),
                  pl.BlockSpec((N_EXPERTS, 1), lambda p, i: (0, 0))],
        out_specs=[pl.BlockSpec((8, tt), lambda p, i: (0, i * p)),
                   pl.BlockSpec((8, tt), lambda p, i: (0, i * p)),
                   pl.BlockSpec((2, N_EXPERTS, 128), lambda p, i: (0, 0, 0)),
                   pl.BlockSpec((8, n_blocks), lambda p, i: (0, 0))],
        scratch_shapes=[pltpu.VMEM((N_EXPERTS, 128), f32), pltpu.VMEM((N_EXPERTS, 128), f32),
                        pltpu.VMEM((N_EXPERTS, 128), f32), pltpu.VMEM((tt, tt), bf16)],
        compiler_params=_cparams(("arbitrary", "arbitrary")),
        name="route",
    )(logits_t, router_bias.reshape(N_EXPERTS, 1))


def _dispatch_kernel(pos_ref, cnt_ref, base_ref, h2p_ref, xs_ref, sem, *, n_tiles):
    i = pl.program_id(0)
    tok0 = i * MOE_TM

    def row_copy(src_row, dst_row):
        return pltpu.make_async_copy(h2p_ref.at[pl.ds(src_row, 1)], xs_ref.at[pl.ds(dst_row, 1)], sem)

    n_tok = pos_ref.shape[0] // TOP_K

    def issue(t, carry):
        for k in range(TOP_K):
            row_copy(tok0 + t, pos_ref[k * n_tok + tok0 + t]).start()
        return carry

    lax.fori_loop(0, MOE_TM, issue, 0)

    per_step = -(-N_EXPERTS // n_tiles)
    pads = []
    for e_local in range(per_step):
        e = i * per_step + e_local
        e_c = jnp.minimum(e, N_EXPERTS - 1)
        cnt = cnt_ref[e_c]
        n_pad = jnp.where(e < N_EXPERTS, (MOE_TM - cnt % MOE_TM) % MOE_TM, 0)
        first = base_ref[e_c] + cnt
        pads.append(n_pad)

        def pad_issue(r, carry, first=first):
            row_copy(0, first + r).start()
            return carry

        lax.fori_loop(0, n_pad, pad_issue, 0)

    def drain(t, carry):
        row_copy(0, 0).wait()
        return carry

    total = MOE_TM * TOP_K
    for n_pad in pads:
        total = total + n_pad
    lax.fori_loop(0, total, drain, 0)


def _dispatch(pos_flat, cnt, base, h2p, n_rows):
    n_tok, half = h2p.shape
    n_tiles = n_tok // MOE_TM
    smem = pl.BlockSpec(memory_space=pltpu.SMEM)
    return pl.pallas_call(
        functools.partial(_dispatch_kernel, n_tiles=n_tiles),
        out_shape=jax.ShapeDtypeStruct((n_rows, half), u32),
        grid=(n_tiles,),
        in_specs=[smem, smem, smem, pl.BlockSpec(memory_space=pl.ANY)],
        out_specs=pl.BlockSpec(memory_space=pl.ANY),
        scratch_shapes=[pltpu.SemaphoreType.DMA(())],
        compiler_params=_cparams(("arbitrary",)),
        name="dispatch",
    )(pos_flat, cnt, base, h2p)


def _experts_kernel(blk_e_ref, nblk_ref, xs_ref, w1_ref, w3_ref, w2_ref, ys_ref, w1_scr, w3_scr, w2_scr):
    j = pl.program_id(0)
    n_valid = nblk_ref[0]
    e = blk_e_ref[jnp.minimum(j, n_valid - 1)]
    e_prev = blk_e_ref[jnp.maximum(j - 1, 0)]

    @pl.when((j == 0) | ((j < n_valid) & (e != e_prev)))
    def _():
        w1_scr[...] = w1_ref[0].astype(bf16)
        w3_scr[...] = w3_ref[0].astype(bf16)
        w2_scr[...] = w2_ref[0].astype(bf16)

    @pl.when(j < n_valid)
    def _():
        lo, hi = _unpack_pair(xs_ref[...])
        xlo, xhi = lo.astype(bf16), hi.astype(bf16)
        half = xlo.shape[1]
        h1 = (jnp.dot(xlo, w1_scr[0:half, :], preferred_element_type=f32)
              + jnp.dot(xhi, w1_scr[half:, :], preferred_element_type=f32))
        h3 = (jnp.dot(xlo, w3_scr[0:half, :], preferred_element_type=f32)
              + jnp.dot(xhi, w3_scr[half:, :], preferred_element_type=f32))
        act = (h1 * jax.nn.sigmoid(h1) * h3).astype(bf16)
        y = jnp.dot(act, w2_scr[...], preferred_element_type=f32)
        ys_ref[...] = _pack_pair(y[:, :half], y[:, half:])


def _experts(blk_expert, nblk, xs, w1, w3, w2):
    n_rows, half = xs.shape
    n_blocks = n_rows // MOE_TM
    d, de = w1.shape[1], w1.shape[2]

    def row_blk(j, be, nb):
        return (jnp.minimum(j, nb[0] - 1), 0)

    def w_blk(j, be, nb):
        return (be[jnp.minimum(j, nb[0] - 1)], 0, 0)

    return pl.pallas_call(
        _experts_kernel,
        out_shape=jax.ShapeDtypeStruct((n_rows, half), u32),
        grid_spec=pltpu.PrefetchScalarGridSpec(
            num_scalar_prefetch=2,
            grid=(n_blocks,),
            in_specs=[pl.BlockSpec((MOE_TM, half), row_blk),
                      pl.BlockSpec((1, d, de), w_blk),
                      pl.BlockSpec((1, d, de), w_blk),
                      pl.BlockSpec((1, de, d), w_blk)],
            out_specs=pl.BlockSpec((MOE_TM, half), row_blk),
            scratch_shapes=[pltpu.VMEM((d, de), bf16), pltpu.VMEM((d, de), bf16), pltpu.VMEM((de, d), bf16)]),
        compiler_params=_cparams(("arbitrary",)),
        name="experts",
    )(blk_expert, nblk, xs, w1, w3, w2)


def _combine_kernel(pos_ref, ys_ref, h2p_ref, wts_ref, x1_ref, g2_ref, ws1_ref, ws3_ref, ws2_ref, o_ref,
                    ybuf, sem):
    i = pl.program_id(0)
    tok0 = i * MOE_TM
    n_tok = pos_ref.shape[0] // TOP_K

    def row_copy(src_row, k, t):
        return pltpu.make_async_copy(ys_ref.at[pl.ds(src_row, 1)], ybuf.at[k, pl.ds(t, 1)], sem)

    def issue(t, carry):
        for k in range(TOP_K):
            row_copy(pos_ref[k * n_tok + tok0 + t], k, t).start()
        return carry

    lax.fori_loop(0, MOE_TM, issue, 0)

    lo, hi = _unpack_pair(h2p_ref[...])
    xlo, xhi = lo.astype(bf16), hi.astype(bf16)
    half = xlo.shape[1]
    s1 = (jnp.dot(xlo, ws1_ref[0:half, :], preferred_element_type=f32)
          + jnp.dot(xhi, ws1_ref[half:, :], preferred_element_type=f32))
    s3 = (jnp.dot(xlo, ws3_ref[0:half, :], preferred_element_type=f32)
          + jnp.dot(xhi, ws3_ref[half:, :], preferred_element_type=f32))
    act = (s1 * jax.nn.sigmoid(s1) * s3).astype(bf16)
    shared = jnp.dot(act, ws2_ref[...], preferred_element_type=f32)

    def drain(t, carry):
        for k in range(TOP_K):
            row_copy(0, k, t).wait()
        return carry

    lax.fori_loop(0, MOE_TM, drain, 0)

    w = wts_ref[...]
    r_lo = jnp.zeros((MOE_TM, half), f32)
    r_hi = jnp.zeros((MOE_TM, half), f32)
    for k in range(TOP_K):
        y_lo, y_hi = _unpack_pair(ybuf[k])
        r_lo = r_lo + w[:, k:k + 1] * y_lo
        r_hi = r_hi + w[:, k:k + 1] * y_hi
    g2 = g2_ref[0]
    o_ref[:, 0:half] = x1_ref[:, 0:half] + g2[:, 0:half] * (r_lo + shared[:, 0:half])
    o_ref[:, half:] = x1_ref[:, half:] + g2[:, half:] * (r_hi + shared[:, half:])


def _combine(pos_flat, ys, h2p, wts_t, x1, g2, ws1, ws3, ws2, tiles_per_batch):
    n_tok, d = x1.shape
    half = d // 2
    vec = lambda i: (0, 0)
    return pl.pallas_call(
        _combine_kernel,
        out_shape=jax.ShapeDtypeStruct((n_tok, d), f32),
        grid=(n_tok // MOE_TM,),
        in_specs=[pl.BlockSpec(memory_space=pltpu.SMEM),
                  pl.BlockSpec(memory_space=pl.ANY),
                  pl.BlockSpec((MOE_TM, half), lambda i: (i, 0)),
                  pl.BlockSpec((MOE_TM, 8), lambda i: (i, 0)),
                  pl.BlockSpec((MOE_TM, d), lambda i: (i, 0)),
                  pl.BlockSpec((1, 1, d), lambda i: (i // tiles_per_batch, 0, 0)),
                  pl.BlockSpec(ws1.shape, vec),
                  pl.BlockSpec(ws3.shape, vec),
                  pl.BlockSpec(ws2.shape, vec)],
        out_specs=pl.BlockSpec((MOE_TM, d), lambda i: (i, 0)),
        scratch_shapes=[pltpu.VMEM((TOP_K, MOE_TM, half), u32), pltpu.SemaphoreType.DMA(())],
        compiler_params=_cparams(("arbitrary",)),
        name="combine",
    )(pos_flat, ys, h2p, wts_t, x1, g2, ws1, ws3, ws2)


def _layer(x, ctx, mod, norm1_g, norm2_g, w_in, q_norm_g, k_norm_g, na_rpb, up_f, bias_f, up_b, bias_b, gla_norm_g,
           w_out, router_w, router_bias, w1, w3, w2, ws1, ws3, ws2):
    batch, l, d = x.shape
    lc = ctx.shape[1]
    n_tok = batch * l
    sh1, sc1, g1, sh2, sc2, g2 = [mod[:batch, k * d:(k + 1) * d].reshape(batch, 1, d) for k in range(6)]
    csh1, csc1 = [mod[batch:batch + 1, k * d:(k + 1) * d].reshape(1, 1, d) for k in range(2)]

    w_bf = w_in[:, :COL_DD].astype(bf16)
    wdd_bf = jnp.zeros((d, 128), f32).at[:, :2 * GLA_GATE_RANK].set(w_in[:, COL_DD:]).astype(bf16)
    qg, kg = q_norm_g.reshape(1, -1), k_norm_g.reshape(1, -1)
    tabs = _rope_tables(l)
    n1 = norm1_g.reshape(1, d)
    tm_lat = min(512, l)
    p_lat, dd_lat = _inproj(x.reshape(n_tok, d), sc1, sh1, n1, w_bf, wdd_bf, qg, kg, tabs, tm=tm_lat,
                            tiles_per_group=l // tm_lat, rope=True)
    tm_ctx = batch * lc
    ctabs = tuple(t[:tm_ctx] for t in tabs)
    p_ctx, dd_ctx = _inproj(ctx.reshape(batch * lc, d), csc1, csh1, n1, w_bf, wdd_bf, qg, kg, ctabs, tm=tm_ctx,
                            tiles_per_group=1, rope=False)

    bias = _na_bias_tables(na_rpb, l // GRID_W)
    o_na = _na(p_lat, p_ctx, bias, batch, l, lc)

    o_f, o_b = _gla(p_lat, dd_lat, p_ctx, dd_ctx, _pad_gate_up(up_f, 0), _pad_gate_up(up_b, GLA_GATE_RANK),
                    bias_f.reshape(1, -1), bias_b.reshape(1, -1), batch, l, lc)

    tm_out = 256
    x1, h2p, logits_t = _outproj(o_na, o_f, o_b, p_lat, gla_norm_g.reshape(1, -1), w_out.astype(bf16),
                                 x.reshape(n_tok, d), g1, norm2_g.reshape(1, d), sc2, sh2, router_w.T,
                                 l // tm_out, tm_out)

    n_blocks = n_tok * TOP_K // MOE_TM + N_EXPERTS
    pos, wts, meta, blk = _route(logits_t, router_bias, n_blocks)
    pos_flat = pos[:TOP_K].reshape(-1)
    xs = _dispatch(pos_flat, meta[0, :, 0], meta[1, :, 0], h2p, n_blocks * MOE_TM)
    ys = _experts(blk[0], blk[1, :1], xs, w1, w3, w2)
    out = _combine(pos_flat, ys, h2p, wts.T, x1, g2, ws1.astype(bf16), ws3.astype(bf16), ws2.astype(bf16),
                   l // MOE_TM)
    return out.reshape(batch, l, d)


def kernel(x, c, ctx, c_ctx, w_mod, b_mod, norm1_g, norm2_g, w_in, q_norm_g, k_norm_g, na_rpb, gla_gate_up_f,
           gla_gate_bias_f, gla_gate_up_b, gla_gate_bias_b, gla_norm_g, w_out, router_w, router_bias, expert_w1,
           expert_w3, expert_w2, shared_w1, shared_w3, shared_w2):
    depth = w_mod.shape[0]
    assert depth == 1, "context-stream outputs are only dropped for a single (last) layer"
    batch, d = c.shape
    cond8 = jnp.zeros((8, d), f32).at[:batch].set(c).at[batch].set(c_ctx)
    mod = _adaln(cond8, w_mod[0], b_mod[0])
    return _layer(x, ctx, mod, norm1_g[0], norm2_g[0], w_in[0], q_norm_g[0], k_norm_g[0], na_rpb[0],
                  gla_gate_up_f[0], gla_gate_bias_f[0], gla_gate_up_b[0], gla_gate_bias_b[0], gla_norm_g[0],
                  w_out[0], router_w[0], router_bias[0], expert_w1[0], expert_w3[0], expert_w2[0], shared_w1[0],
                  shared_w3[0], shared_w2[0])
```

```python
import functools

import jax
import jax.numpy as jnp
from jax import lax
from jax.experimental import pallas as pl
from jax.experimental.pallas import tpu as pltpu

f32 = jnp.float32
bf16 = jnp.bfloat16
i32 = jnp.int32

GRID_W = 64
NORM_EPS = 1e-6
NA_HEADS = 8
NA_HEAD_DIM = 128
NA_WIN_H = 8
NA_WIN_W = 16
GLA_HEADS = 4
GLA_DK = 128
GLA_DV = 256
GLA_GATE_RANK = 16
GLA_GATE_TAU = 16.0
GLA_CHUNK = 64
GLA_SUB = 16
ROPE_BASE = 10000.0
N_EXPERTS = 64
N_GROUPS = 8
TOPK_GROUPS = 4
TOP_K = 6
D_EXPERT = 512
ROUTED_SCALE = 2.5

NA_WIDTH = NA_HEADS * NA_HEAD_DIM
GLA_KEY_WIDTH = GLA_HEADS * GLA_DK
GLA_VAL_WIDTH = GLA_HEADS * GLA_DV
COL_NQ, COL_NK, COL_NV = 0, NA_WIDTH, 2 * NA_WIDTH
COL_GQ = 3 * NA_WIDTH
COL_GK = COL_GQ + GLA_KEY_WIDTH
COL_GV = COL_GK + GLA_KEY_WIDTH
COL_GATE = COL_GV + GLA_VAL_WIDTH
COL_DD = COL_GATE + GLA_VAL_WIDTH
PROJ_TN = 512
NA_QROWS = 4
NA_KROWS = 12
MOE_TM = 256
NEG = -1e30

VMEM_LIMIT = 56 * 1024 * 1024


def _cparams(sem, vmem=VMEM_LIMIT):
    return pltpu.CompilerParams(dimension_semantics=sem, vmem_limit_bytes=vmem)


NT_DIMS = (((1,), (1,)), ((), ()))
TN_DIMS = (((0,), (0,)), ((), ()))


TOKEN_TILE = (16, 128)


def _store_token_tiles(ref, rows2d):
    for s in range(TOKEN_TILE[0]):
        ref[:, s, :] = rows2d[:, s * TOKEN_TILE[1]:(s + 1) * TOKEN_TILE[1]]


def _load_token_tiles(ref, lead=()):
    return jnp.concatenate([ref[lead + (slice(None), s, slice(None))] for s in range(TOKEN_TILE[0])], axis=-1)


def _mod_kernel(c_ref, w_ref, b_ref, o_ref):
    c = c_ref[...]
    s = (c * jax.nn.sigmoid(c)).astype(bf16)
    o_ref[...] = jnp.dot(s, w_ref[...].astype(bf16), preferred_element_type=f32) + b_ref[...]


def _adaln(cond8, w_mod, b_mod):
    d, n = w_mod.shape
    tn = 1024
    return pl.pallas_call(
        _mod_kernel,
        out_shape=jax.ShapeDtypeStruct((8, n), f32),
        grid=(n // tn,),
        in_specs=[pl.BlockSpec((8, d), lambda j: (0, 0)),
                  pl.BlockSpec((d, tn), lambda j: (0, j)),
                  pl.BlockSpec((1, tn), lambda j: (0, j))],
        out_specs=pl.BlockSpec((8, tn), lambda j: (0, j)),
        compiler_params=_cparams(("arbitrary",)),
        name="mod",
    )(cond8, w_mod, b_mod.reshape(1, n))


def _head_rms(a, g, width):
    parts = []
    for h in range(a.shape[-1] // width):
        ah = a[:, h * width:(h + 1) * width]
        ms = jnp.mean(ah * ah, axis=-1, keepdims=True)
        parts.append(ah * lax.rsqrt(ms + NORM_EPS) * g)
    return jnp.concatenate(parts, axis=-1)


def _rope(a, cos, sin_lo, sin_hi):
    parts = []
    for h in range(a.shape[-1] // GLA_DK):
        ah = a[:, h * GLA_DK:(h + 1) * GLA_DK]
        parts.append(ah * cos + pltpu.roll(ah, 32, axis=1) * sin_hi + pltpu.roll(ah, 96, axis=1) * sin_lo)
    return jnp.concatenate(parts, axis=-1)


def _inproj_kernel(x_ref, sc_ref, sh_ref, g_ref, w_ref, wdd_ref, qg_ref, kg_ref, cos_ref, slo_ref, shi_ref,
                   o_ref, dd_ref, h_scr, *, rope):
    j = pl.program_id(1)

    @pl.when(j == 0)
    def _():
        x = x_ref[...]
        ms = jnp.mean(x * x, axis=-1, keepdims=True)
        y = x * lax.rsqrt(ms + NORM_EPS) * g_ref[...]
        hb = (y * (1.0 + sc_ref[0]) + sh_ref[0]).astype(bf16)
        h_scr[...] = hb
        dd_ref[...] = jnp.dot(hb, wdd_ref[...], preferred_element_type=f32)

    acc = jnp.dot(h_scr[...], w_ref[...], preferred_element_type=f32)
    jq, jk, jgq, jgk, jgate = (COL_NQ // PROJ_TN, COL_NK // PROJ_TN, COL_GQ // PROJ_TN, COL_GK // PROJ_TN,
                               COL_GATE // PROJ_TN)

    @pl.when(j < jk)
    def _():
        o_ref[...] = _head_rms(acc, qg_ref[...], NA_HEAD_DIM).astype(bf16)

    @pl.when((j >= jk) & (j < COL_NV // PROJ_TN))
    def _():
        o_ref[...] = _head_rms(acc, kg_ref[...], NA_HEAD_DIM).astype(bf16)

    @pl.when(j == jgq)
    def _():
        a = _rope(acc, cos_ref[...], slo_ref[...], shi_ref[...]) if rope else acc
        o_ref[...] = (a * (GLA_DK ** -0.5)).astype(bf16)

    @pl.when(j == jgk)
    def _():
        a = _rope(acc, cos_ref[...], slo_ref[...], shi_ref[...]) if rope else acc
        o_ref[...] = a.astype(bf16)

    @pl.when(j >= jgate)
    def _():
        o_ref[...] = (acc * jax.nn.sigmoid(acc)).astype(bf16)

    @pl.when(((j >= COL_NV // PROJ_TN) & (j < jgq)) | ((j > jgk) & (j < jgate)))
    def _():
        o_ref[...] = acc.astype(bf16)


def _inproj(x2d, sc, sh, norm_g, w_bf, wdd_bf, qg, kg, rope_tabs, *, tm, tiles_per_group, rope):
    m, d = x2d.shape
    n = w_bf.shape[1]
    cos, slo, shi = rope_tabs
    rope_tiles = cos.shape[0] // tm
    grid = (m // tm, n // PROJ_TN)
    vec = lambda i, j: (0, 0)
    return pl.pallas_call(
        functools.partial(_inproj_kernel, rope=rope),
        out_shape=[jax.ShapeDtypeStruct((m, n), bf16), jax.ShapeDtypeStruct((m, 128), f32)],
        grid=grid,
        in_specs=[pl.BlockSpec((tm, d), lambda i, j: (i, 0)),
                  pl.BlockSpec((1, 1, d), lambda i, j: (i // tiles_per_group, 0, 0)),
                  pl.BlockSpec((1, 1, d), lambda i, j: (i // tiles_per_group, 0, 0)),
                  pl.BlockSpec((1, d), vec),
                  pl.BlockSpec((d, PROJ_TN), lambda i, j: (0, j)),
                  pl.BlockSpec((d, 128), vec),
                  pl.BlockSpec((1, NA_HEAD_DIM), vec),
                  pl.BlockSpec((1, NA_HEAD_DIM), vec),
                  pl.BlockSpec((tm, GLA_DK), lambda i, j: (i % rope_tiles, 0)),
                  pl.BlockSpec((tm, GLA_DK), lambda i, j: (i % rope_tiles, 0)),
                  pl.BlockSpec((tm, GLA_DK), lambda i, j: (i % rope_tiles, 0))],
        out_specs=[pl.BlockSpec((tm, PROJ_TN), lambda i, j: (i, j)),
                   pl.BlockSpec((tm, 128), lambda i, j: (i, 0))],
        scratch_shapes=[pltpu.VMEM((tm, d), bf16)],
        compiler_params=_cparams(("arbitrary", "arbitrary")),
        name="inproj_rope" if rope else "inproj_ctx",
    )(x2d, sc, sh, norm_g, w_bf, wdd_bf, qg, kg, cos, slo, shi)


def _rope_tables(l):
    t = jnp.arange(l)
    half = GLA_DK // 4
    inv_freq = ROPE_BASE ** (-jnp.arange(half, dtype=f32) / half)
    ang_r = (t // GRID_W).astype(f32)[:, None] * inv_freq[None, :]
    ang_c = (t % GRID_W).astype(f32)[:, None] * inv_freq[None, :]
    cos = jnp.concatenate([jnp.cos(ang_r)] * 2 + [jnp.cos(ang_c)] * 2, axis=-1)
    sin = jnp.concatenate([jnp.sin(ang_r)] * 2 + [jnp.sin(ang_c)] * 2, axis=-1)
    lane = jnp.arange(GLA_DK) % (2 * half)
    sin_lo = jnp.where(lane[None, :] < half, -sin, 0.0)
    sin_hi = jnp.where(lane[None, :] >= half, sin, 0.0)
    return cos, sin_lo, sin_hi


def _na_kernel(q_ref, k_ref, v_ref, kc_ref, vc_ref, bias_ref, o_ref, *, rows):
    a = pl.program_id(2)
    start = pl.multiple_of(jnp.clip(NA_QROWS * a - NA_WIN_H // 2, 0, rows - NA_KROWS) * GRID_W, GRID_W)
    scale = NA_HEAD_DIM ** -0.5
    q = q_ref[...]
    k = k_ref[pl.ds(start, NA_KROWS * GRID_W), :]
    v = v_ref[pl.ds(start, NA_KROWS * GRID_W), :]
    s_loc = lax.dot_general(q, k, NT_DIMS, preferred_element_type=f32) * scale + bias_ref[0, 0]
    s_ctx = lax.dot_general(q, kc_ref[...], NT_DIMS, preferred_element_type=f32) * scale
    m = jnp.maximum(jnp.max(s_loc, axis=-1, keepdims=True), jnp.max(s_ctx, axis=-1, keepdims=True))
    p_loc = jnp.exp(s_loc - m)
    p_ctx = jnp.exp(s_ctx - m)
    denom = jnp.sum(p_loc, axis=-1, keepdims=True) + jnp.sum(p_ctx, axis=-1, keepdims=True)
    o = (jnp.dot(p_loc.astype(bf16), v, preferred_element_type=f32)
         + jnp.dot(p_ctx.astype(bf16), vc_ref[...], preferred_element_type=f32))
    o_ref[...] = (o / denom).astype(bf16)


def _na_bias_tables(rpb, rows):
    n_h, n_dr, _ = rpb.shape
    n_a = rows // NA_QROWS
    band_rows = []
    for c in range(GRID_W):
        cs = min(max(c - NA_WIN_W // 2, 0), GRID_W - NA_WIN_W)
        d0 = cs - c + NA_WIN_W - 1
        band_rows.append(jnp.pad(rpb[:, :, d0:d0 + NA_WIN_W], ((0, 0), (0, 0), (cs, GRID_W - NA_WIN_W - cs)),
                                 constant_values=NEG))
    band = jnp.stack(band_rows, axis=2)
    masked = jnp.full((n_h, GRID_W, GRID_W), NEG, f32)
    tabs = []
    for a in (0, 1, n_a - 1):
        start = min(max(NA_QROWS * a - NA_WIN_H // 2, 0), rows - NA_KROWS)
        q_blocks = []
        for qr in range(NA_QROWS):
            r_abs = NA_QROWS * a + qr
            rs = min(max(r_abs - NA_WIN_H // 2, 0), rows - NA_WIN_H)
            k_blocks = []
            for kr in range(NA_KROWS):
                k_abs = start + kr
                ok = rs <= k_abs < rs + NA_WIN_H
                k_blocks.append(band[:, k_abs - r_abs + NA_WIN_H - 1] if ok else masked)
            q_blocks.append(jnp.concatenate(k_blocks, axis=-1))
        tabs.append(jnp.concatenate(q_blocks, axis=-2))
    return jnp.stack(tabs, axis=1).astype(f32)


def _na(p_lat, p_ctx, bias, batch, l, lc):
    rows = l // GRID_W
    n_a = rows // NA_QROWS
    tq = NA_QROWS * GRID_W
    hk, hv = COL_NK // NA_HEAD_DIM, COL_NV // NA_HEAD_DIM

    def variant(a):
        return jnp.where(a == 0, 0, jnp.where(a == n_a - 1, 2, 1))

    return pl.pallas_call(
        functools.partial(_na_kernel, rows=rows),
        out_shape=jax.ShapeDtypeStruct((batch * l, NA_WIDTH), bf16),
        grid=(batch, NA_HEADS, n_a),
        in_specs=[pl.BlockSpec((tq, NA_HEAD_DIM), lambda b, h, a: (b * n_a + a, h)),
                  pl.BlockSpec((l, NA_HEAD_DIM), lambda b, h, a: (b, hk + h)),
                  pl.BlockSpec((l, NA_HEAD_DIM), lambda b, h, a: (b, hv + h)),
                  pl.BlockSpec((lc, NA_HEAD_DIM), lambda b, h, a: (b, hk + h)),
                  pl.BlockSpec((lc, NA_HEAD_DIM), lambda b, h, a: (b, hv + h)),
                  pl.BlockSpec((1, 1, tq, NA_KROWS * GRID_W), lambda b, h, a: (h, variant(a), 0, 0))],
        out_specs=pl.BlockSpec((tq, NA_HEAD_DIM), lambda b, h, a: (b * n_a + a, h)),
        compiler_params=_cparams(("arbitrary", "arbitrary", "arbitrary")),
        name="na",
    )(p_lat, p_lat, p_lat, p_ctx, p_ctx, bias)


def _log_decay(dd, up, bias):
    z = jnp.dot(dd.astype(bf16), up, preferred_element_type=f32) + bias
    return (jnp.minimum(z, 0.0) - jnp.log1p(jnp.exp(-jnp.abs(z)))) * (1.0 / GLA_GATE_TAU)


def _gla_chunk(q, k, v, g, st, *, rev, need_out):
    c, sub = GLA_CHUNK, GLA_SUB
    r = lax.broadcasted_iota(i32, (c, c), 0)
    col = lax.broadcasted_iota(i32, (c, c), 1)
    tri = ((r <= col) if rev else (r >= col)).astype(f32)
    cum = jnp.dot(tri, g, precision=lax.Precision.HIGHEST, preferred_element_type=f32)
    total = cum[0:1] if rev else cum[c - 1:c]
    kd = (k * jnp.exp(total - cum)).astype(bf16)
    st_new = st * jnp.exp(total) + lax.dot_general(v, kd, TN_DIMS, preferred_element_type=f32)
    if not need_out:
        return None, st_new
    qd = (q * jnp.exp(cum)).astype(bf16)
    o_inter = lax.dot_general(qd, st.astype(bf16), NT_DIMS, preferred_element_type=f32)
    t_io = lax.broadcasted_iota(i32, (sub, 1), 0)
    outs = []
    for i in range(c // sub):
        lo, hi = i * sub, (i + 1) * sub
        q_i, k_i, cum_i, v_i = q[lo:hi], k[lo:hi], cum[lo:hi], v[lo:hi].astype(f32)
        o_i = jnp.zeros((sub, GLA_DV), f32)
        klo, khi = (hi, c) if rev else (0, lo)
        if khi > klo:
            edge = cum[hi:hi + 1] if rev else cum[lo - 1:lo]
            qe = (q_i * jnp.exp(cum_i - edge)).astype(bf16)
            ke = (k[klo:khi] * jnp.exp(edge - cum[klo:khi])).astype(bf16)
            att = lax.dot_general(qe, ke, NT_DIMS, preferred_element_type=f32)
            o_i = o_i + jnp.dot(att.astype(bf16), v[klo:khi], preferred_element_type=f32)
        for s in range(sub):
            ok = (t_io <= s) if rev else (t_io >= s)
            e = jnp.exp(jnp.where(ok, cum_i - cum_i[s:s + 1], NEG))
            a_col = jnp.sum(q_i * k_i[s:s + 1] * e, axis=-1, keepdims=True)
            o_i = o_i + a_col * v_i[s:s + 1]
        outs.append(o_i)
    return o_inter + jnp.concatenate(outs, axis=0), st_new


def _gla_kernel(qf_ref, kf_ref, vf_ref, ddf_ref, qb_ref, kb_ref, vb_ref, ddb_ref, kc_ref, vc_ref, ddc_ref,
                upf_ref, upb_ref, bf_ref, bb_ref, of_ref, ob_ref, sf_scr, sb_scr):
    j = pl.program_id(2)
    n_chunks = kc_ref.shape[0] // GLA_CHUNK
    upf, upb, bias_f, bias_b = upf_ref[0], upb_ref[0], bf_ref[...], bb_ref[...]

    def rows(ref, ci):
        return ref[ci * GLA_CHUNK:(ci + 1) * GLA_CHUNK, :]

    @pl.when(j == 0)
    def _():
        sf = jnp.zeros((GLA_DV, GLA_DK), f32)
        sb = jnp.zeros((GLA_DV, GLA_DK), f32)
        for ci in range(n_chunks):
            cb = n_chunks - 1 - ci
            gf = _log_decay(rows(ddc_ref, ci), upf, bias_f)
            gb = _log_decay(rows(ddc_ref, cb), upb, bias_b)
            _, sf = _gla_chunk(None, rows(kc_ref, ci).astype(f32), rows(vc_ref, ci), gf, sf, rev=False,
                               need_out=False)
            _, sb = _gla_chunk(None, rows(kc_ref, cb).astype(f32), rows(vc_ref, cb), gb, sb, rev=True,
                               need_out=False)
        sf_scr[...] = sf
        sb_scr[...] = sb

    @pl.when(j > 0)
    def _():
        sf = sf_scr[...]
        sb = sb_scr[...]
        nc = kf_ref.shape[0] // GLA_CHUNK
        for ci in range(nc):
            cb = nc - 1 - ci
            gf = _log_decay(rows(ddf_ref, ci), upf, bias_f)
            gb = _log_decay(rows(ddb_ref, cb), upb, bias_b)
            o_f, sf = _gla_chunk(rows(qf_ref, ci).astype(f32), rows(kf_ref, ci).astype(f32), rows(vf_ref, ci), gf,
                                 sf, rev=False, need_out=True)
            o_b, sb = _gla_chunk(rows(qb_ref, cb).astype(f32), rows(kb_ref, cb).astype(f32), rows(vb_ref, cb), gb,
                                 sb, rev=True, need_out=True)
            of_ref[ci * GLA_CHUNK:(ci + 1) * GLA_CHUNK, :] = o_f.astype(bf16)
            ob_ref[cb * GLA_CHUNK:(cb + 1) * GLA_CHUNK, :] = o_b.astype(bf16)
        sf_scr[...] = sf
        sb_scr[...] = sb


def _gla(p_lat, dd_lat, p_ctx, dd_ctx, upf, upb, bias_f, bias_b, batch, l, lc):
    tt = lc
    n_l = l // tt
    hq, hk, hv = COL_GQ // GLA_DK, COL_GK // GLA_DK, COL_GV // GLA_DV

    def fwd(b, h, j):
        return b * n_l + jnp.maximum(j - 1, 0)

    def bwd(b, h, j):
        return b * n_l + n_l - jnp.maximum(j, 1)

    lat = lambda rowmap, width, col: pl.BlockSpec((tt, width), lambda b, h, j: (rowmap(b, h, j), col + h))
    ddspec = lambda rowmap: pl.BlockSpec((tt, 128), lambda b, h, j: (rowmap(b, h, j), 0))
    ctx = lambda width, col: pl.BlockSpec((tt, width), lambda b, h, j: (b, col + h))
    return pl.pallas_call(
        _gla_kernel,
        out_shape=[jax.ShapeDtypeStruct((batch * l, GLA_VAL_WIDTH), bf16)] * 2,
        grid=(batch, GLA_HEADS, n_l + 1),
        in_specs=[lat(fwd, GLA_DK, hq), lat(fwd, GLA_DK, hk), lat(fwd, GLA_DV, hv), ddspec(fwd),
                  lat(bwd, GLA_DK, hq), lat(bwd, GLA_DK, hk), lat(bwd, GLA_DV, hv), ddspec(bwd),
                  ctx(GLA_DK, hk), ctx(GLA_DV, hv), pl.BlockSpec((tt, 128), lambda b, h, j: (b, 0)),
                  pl.BlockSpec((1, 128, GLA_DK), lambda b, h, j: (h, 0, 0)),
                  pl.BlockSpec((1, 128, GLA_DK), lambda b, h, j: (h, 0, 0)),
                  pl.BlockSpec((1, GLA_DK), lambda b, h, j: (0, h)),
                  pl.BlockSpec((1, GLA_DK), lambda b, h, j: (0, h))],
        out_specs=[pl.BlockSpec((tt, GLA_DV), lambda b, h, j: (fwd(b, h, j), h)),
                   pl.BlockSpec((tt, GLA_DV), lambda b, h, j: (bwd(b, h, j), h))],
        scratch_shapes=[pltpu.VMEM((GLA_DV, GLA_DK), f32), pltpu.VMEM((GLA_DV, GLA_DK), f32)],
        compiler_params=_cparams(("arbitrary", "arbitrary", "arbitrary")),
        name="gla",
    )(p_lat, p_lat, p_lat, dd_lat, p_lat, p_lat, p_lat, dd_lat, p_ctx, p_ctx, dd_ctx, upf, upb, bias_f, bias_b)


def _pad_gate_up(up, row0):
    w = up.reshape(GLA_GATE_RANK, GLA_HEADS, GLA_DK).transpose(1, 0, 2)
    return jnp.zeros((GLA_HEADS, 128, GLA_DK), f32).at[:, row0:row0 + GLA_GATE_RANK].set(w).astype(bf16)


def _outproj_kernel(ona_ref, of_ref, ob_ref, gate_ref, gng_ref, w_ref, x_ref, g1_ref, n2g_ref, sc2_ref, sh2_ref,
                    rwt_ref, x1_ref, h2t_ref, lg_ref):
    o = of_ref[...].astype(f32) + ob_ref[...].astype(f32)
    gla = (_head_rms(o, gng_ref[...], GLA_DV) * gate_ref[...].astype(f32)).astype(bf16)
    acc = (jnp.dot(ona_ref[...], w_ref[0:NA_WIDTH, :], preferred_element_type=f32)
           + jnp.dot(gla, w_ref[NA_WIDTH:, :], preferred_element_type=f32))
    x1 = x_ref[...] + g1_ref[0] * acc
    x1_ref[...] = x1
    ms = jnp.mean(x1 * x1, axis=-1, keepdims=True)
    h2 = x1 * lax.rsqrt(ms + NORM_EPS) * n2g_ref[...] * (1.0 + sc2_ref[0]) + sh2_ref[0]
    lg_ref[...] = lax.dot_general(rwt_ref[...], h2, NT_DIMS, precision=lax.Precision.HIGHEST,
                                  preferred_element_type=f32)
    _store_token_tiles(h2t_ref, h2.astype(bf16))


def _outproj(o_na, o_f, o_b, p_lat, gnorm_g, w_out_bf, x2d, g1, norm2_g, sc2, sh2, rwt, tiles_per_batch, tm):
    m, d = x2d.shape
    vec = lambda i: (0, 0)
    per_b = lambda i: (i // tiles_per_batch, 0, 0)
    return pl.pallas_call(
        _outproj_kernel,
        out_shape=[jax.ShapeDtypeStruct((m, d), f32), jax.ShapeDtypeStruct((m,) + TOKEN_TILE, bf16),
                   jax.ShapeDtypeStruct((N_EXPERTS, m), f32)],
        grid=(m // tm,),
        in_specs=[pl.BlockSpec((tm, NA_WIDTH), lambda i: (i, 0)),
                  pl.BlockSpec((tm, GLA_VAL_WIDTH), lambda i: (i, 0)),
                  pl.BlockSpec((tm, GLA_VAL_WIDTH), lambda i: (i, 0)),
                  pl.BlockSpec((tm, GLA_VAL_WIDTH), lambda i: (i, COL_GATE // GLA_VAL_WIDTH)),
                  pl.BlockSpec((1, GLA_DV), vec),
                  pl.BlockSpec(w_out_bf.shape, vec),
                  pl.BlockSpec((tm, d), lambda i: (i, 0)),
                  pl.BlockSpec((1, 1, d), per_b),
                  pl.BlockSpec((1, d), vec),
                  pl.BlockSpec((1, 1, d), per_b),
                  pl.BlockSpec((1, 1, d), per_b),
                  pl.BlockSpec((N_EXPERTS, d), vec)],
        out_specs=[pl.BlockSpec((tm, d), lambda i: (i, 0)),
                   pl.BlockSpec((tm,) + TOKEN_TILE, lambda i: (i, 0, 0)),
                   pl.BlockSpec((N_EXPERTS, tm), lambda i: (0, i))],
        compiler_params=_cparams(("arbitrary",)),
        name="outproj",
    )(o_na, o_f, o_b, p_lat, gnorm_g, w_out_bf, x2d, g1, norm2_g, sc2, sh2, rwt)


def _first_argmax(vals, iota, axis, sentinel):
    mx = jnp.max(vals, axis=axis, keepdims=True)
    am = jnp.min(jnp.where(vals == mx, iota, sentinel), axis=axis, keepdims=True)
    return mx, am


def _route_tile(lg, rb):
    t = lg.shape[1]
    per = N_EXPERTS // N_GROUPS
    scores = jax.nn.sigmoid(lg)
    biased = scores + rb
    b3 = biased.reshape(N_GROUPS, per, t)
    io3 = lax.broadcasted_iota(i32, (N_GROUPS, per, t), 1)
    m1, a1 = _first_argmax(b3, io3, 1, per)
    m2 = jnp.max(jnp.where(io3 == a1, -jnp.inf, b3), axis=1, keepdims=True)
    gs = (m1 + m2).reshape(N_GROUPS, t)
    iog = lax.broadcasted_iota(i32, (N_GROUPS, t), 0)
    gsel = jnp.zeros((N_GROUPS, t), jnp.bool_)
    for _ in range(TOPK_GROUPS):
        _, am = _first_argmax(gs, iog, 0, N_GROUPS)
        hit = iog == am
        gsel = gsel | hit
        gs = jnp.where(hit, -jnp.inf, gs)
    emask = jnp.broadcast_to(gsel[:, None, :], (N_GROUPS, per, t)).reshape(N_EXPERTS, t)
    masked = jnp.where(emask, biased, -jnp.inf)
    ioe = lax.broadcasted_iota(i32, (N_EXPERTS, t), 0)
    hits, sels = [], []
    for _ in range(TOP_K):
        _, am = _first_argmax(masked, ioe, 0, N_EXPERTS)
        hit = ioe == am
        hits.append(hit)
        sels.append(jnp.sum(jnp.where(hit, scores, 0.0), axis=0, keepdims=True))
        masked = jnp.where(hit, -jnp.inf, masked)
    denom = sels[0]
    for s in sels[1:]:
        denom = denom + s
    wts = [s / denom * ROUTED_SCALE for s in sels]
    return hits, wts


def _route_kernel(lg_ref, rb_ref, pos_ref, wts_ref, meta_ref, blk_ref, cnt_scr, carry_scr, base_scr, tri_scr):
    ph = pl.program_id(0)
    i = pl.program_id(1)
    t = lg_ref.shape[1]
    hits, wts = _route_tile(lg_ref[...], rb_ref[...])
    sel = hits[0]
    for h in hits[1:]:
        sel = sel | h
    self32 = sel.astype(f32)
    tile_cnt = jnp.sum(self32, axis=1, keepdims=True)

    @pl.when((ph == 0) & (i == 0))
    def _():
        cnt_scr[...] = jnp.zeros_like(cnt_scr)

    @pl.when(ph == 0)
    def _():
        cnt_scr[...] += jnp.broadcast_to(tile_cnt, cnt_scr.shape)

    @pl.when((ph == 1) & (i == 0))
    def _():
        cnt = cnt_scr[...]
        nblk = jnp.floor((cnt + (MOE_TM - 1)) * (1.0 / MOE_TM))
        r = lax.broadcasted_iota(i32, (N_EXPERTS, N_EXPERTS), 0)
        c = lax.broadcasted_iota(i32, (N_EXPERTS, N_EXPERTS), 1)
        excl = jnp.dot((c < r).astype(bf16), nblk.astype(bf16), preferred_element_type=f32)
        base_scr[...] = excl * MOE_TM
        carry_scr[...] = jnp.zeros_like(carry_scr)
        meta_ref[0] = cnt.astype(i32)
        meta_ref[1] = (excl * MOE_TM).astype(i32)
        incl = excl + nblk
        nb = blk_ref.shape[1]
        jj = lax.broadcasted_iota(i32, (N_EXPERTS, nb), 1).astype(f32)
        owner = jnp.sum((jnp.broadcast_to(incl[:, 0:1], (N_EXPERTS, nb)) <= jj).astype(i32), axis=0, keepdims=True)
        blk_ref[0:1, :] = jnp.minimum(owner, N_EXPERTS - 1)
        blk_ref[1:2, :] = jnp.broadcast_to(incl[N_EXPERTS - 1:N_EXPERTS, 0:1].astype(i32), (1, nb))
        blk_ref[2:8, :] = jnp.zeros((6, nb), i32)
        rr = lax.broadcasted_iota(i32, (t, t), 0)
        cc = lax.broadcasted_iota(i32, (t, t), 1)
        tri_scr[...] = (rr < cc).astype(bf16)

    @pl.when(ph == 1)
    def _():
        rank = jnp.dot(sel.astype(bf16), tri_scr[...], preferred_element_type=f32)
        posf = base_scr[:, 0:1] + carry_scr[:, 0:1] + rank
        for k in range(TOP_K):
            pos_ref[k:k + 1, :] = jnp.sum(jnp.where(hits[k], posf, 0.0), axis=0, keepdims=True).astype(i32)
            wts_ref[k:k + 1, :] = wts[k]
        pos_ref[TOP_K:8, :] = jnp.zeros((8 - TOP_K, t), i32)
        wts_ref[TOP_K:8, :] = jnp.zeros((8 - TOP_K, t), f32)
        carry_scr[...] += jnp.broadcast_to(tile_cnt, carry_scr.shape)


def _route(logits_t, router_bias, n_blocks):
    n_tok = logits_t.shape[1]
    tt = 1024
    n_tiles = n_tok // tt
    return pl.pallas_call(
        _route_kernel,
        out_shape=[jax.ShapeDtypeStruct((8, n_tok), i32), jax.ShapeDtypeStruct((8, n_tok), f32),
                   jax.ShapeDtypeStruct((2, N_EXPERTS, 128), i32), jax.ShapeDtypeStruct((8, n_blocks), i32)],
        grid=(2, n_tiles),
        in_specs=[pl.BlockSpec((N_EXPERTS, tt), lambda p, i: (0, i)),
                  pl.BlockSpec((N_EXPERTS, 1), lambda p, i: (0, 0))],
        out_specs=[pl.BlockSpec((8, tt), lambda p, i: (0, i * p)),
                   pl.BlockSpec((8, tt), lambda p, i: (0, i * p)),
                   pl.BlockSpec((2, N_EXPERTS, 128), lambda p, i: (0, 0, 0)),
                   pl.BlockSpec((8, n_blocks), lambda p, i: (0, 0))],
        scratch_shapes=[pltpu.VMEM((N_EXPERTS, 128), f32), pltpu.VMEM((N_EXPERTS, 128), f32),
                        pltpu.VMEM((N_EXPERTS, 128), f32), pltpu.VMEM((tt, tt), bf16)],
        compiler_params=_cparams(("arbitrary", "arbitrary")),
        name="route",
    )(logits_t, router_bias.reshape(N_EXPERTS, 1))


def _experts_kernel(blk_e_ref, nblk_ref, pos_ref, cnt_ref, base_ref, h2t_ref, w1_ref, w3_ref, w2_ref, ys_ref,
                    tok_smem, xbuf, sems, w1_scr, w3_scr, w2_scr):
    j = pl.program_id(0)
    n_valid = nblk_ref[0]
    n_tok = pos_ref.shape[0] // TOP_K

    def row_copy(row, slot, r):
        return pltpu.make_async_copy(h2t_ref.at[tok_smem[row]], xbuf.at[slot, r], sems.at[slot])

    def start_gather(block, slot):
        def issue(r, carry):
            row_copy(block * MOE_TM + r, slot, r).start()
            return carry

        lax.fori_loop(0, MOE_TM, issue, 0, unroll=8)

    @pl.when(j == 0)
    def _():
        def pad_expert(e, carry):
            cnt = cnt_ref[e]
            first = base_ref[e] + cnt

            def fill(r, c):
                tok_smem[first + r] = 0
                return c

            lax.fori_loop(0, (MOE_TM - cnt % MOE_TM) % MOE_TM, fill, 0)
            return carry

        lax.fori_loop(0, N_EXPERTS, pad_expert, 0)

        def invert(t, carry):
            for k in range(TOP_K):
                tok_smem[pos_ref[k * n_tok + t]] = t
            return carry

        lax.fori_loop(0, n_tok, invert, 0, unroll=4)
        start_gather(0, 0)

    @pl.when(j + 1 < n_valid)
    def _():
        start_gather(j + 1, (j + 1) % 2)

    e = blk_e_ref[jnp.minimum(j, n_valid - 1)]
    e_prev = blk_e_ref[jnp.maximum(j - 1, 0)]

    @pl.when((j == 0) | ((j < n_valid) & (e != e_prev)))
    def _():
        w1_scr[...] = w1_ref[0].astype(bf16)
        w3_scr[...] = w3_ref[0].astype(bf16)
        w2_scr[...] = w2_ref[0].astype(bf16)

    @pl.when(j < n_valid)
    def _():
        slot = j % 2

        def drain(r, carry):
            row_copy(0, slot, 0).wait()
            return carry

        lax.fori_loop(0, MOE_TM, drain, 0, unroll=8)
        x = _load_token_tiles(xbuf, (slot,))
        h1 = jnp.dot(x, w1_scr[...], preferred_element_type=f32)
        h3 = jnp.dot(x, w3_scr[...], preferred_element_type=f32)
        act = (h1 * jax.nn.sigmoid(h1) * h3).astype(bf16)
        y = jnp.dot(act, w2_scr[...], preferred_element_type=f32)
        _store_token_tiles(ys_ref, y.astype(bf16))

    @pl.when(j >= n_valid)
    def _():
        ys_ref[...] = jnp.zeros_like(ys_ref)


def _experts(blk_expert, nblk, pos_flat, cnt, base, h2t, w1, w3, w2, n_blocks):
    d, de = w1.shape[1], w1.shape[2]
    n_rows = n_blocks * MOE_TM

    def w_blk(j, be, nb, *_):
        return (be[jnp.minimum(j, nb[0] - 1)], 0, 0)

    return pl.pallas_call(
        _experts_kernel,
        out_shape=jax.ShapeDtypeStruct((n_rows,) + TOKEN_TILE, bf16),
        grid_spec=pltpu.PrefetchScalarGridSpec(
            num_scalar_prefetch=5,
            grid=(n_blocks,),
            in_specs=[pl.BlockSpec(memory_space=pl.ANY),
                      pl.BlockSpec((1, d, de), w_blk),
                      pl.BlockSpec((1, d, de), w_blk),
                      pl.BlockSpec((1, de, d), w_blk)],
            out_specs=pl.BlockSpec((MOE_TM,) + TOKEN_TILE, lambda j, *_: (j, 0, 0)),
            scratch_shapes=[pltpu.SMEM((n_rows,), i32), pltpu.VMEM((2, MOE_TM) + TOKEN_TILE, bf16),
                            pltpu.SemaphoreType.DMA((2,)),
                            pltpu.VMEM((d, de), bf16), pltpu.VMEM((d, de), bf16), pltpu.VMEM((de, d), bf16)]),
        compiler_params=_cparams(("arbitrary",)),
        name="experts",
    )(blk_expert, nblk, pos_flat, cnt, base, h2t, w1, w3, w2)


def _combine_kernel(pos_ref, wts_ref, ys_ref, h2t_ref, x1_ref, g2_ref, ws1_ref, ws3_ref, ws2_ref, o_ref,
                    ybuf, r_scr, sem):
    i = pl.program_id(0)
    tok0 = i * MOE_TM
    n_tok = pos_ref.shape[0] // TOP_K

    def row_copy(src_row, k, t):
        return pltpu.make_async_copy(ys_ref.at[src_row], ybuf.at[k, t], sem)

    def issue(t, carry):
        for k in range(TOP_K):
            row_copy(pos_ref[k * n_tok + tok0 + t], k, t).start()
        return carry

    lax.fori_loop(0, MOE_TM, issue, 0, unroll=2)

    x = _load_token_tiles(h2t_ref)
    s1 = jnp.dot(x, ws1_ref[...], preferred_element_type=f32)
    s3 = jnp.dot(x, ws3_ref[...], preferred_element_type=f32)
    act = (s1 * jax.nn.sigmoid(s1) * s3).astype(bf16)
    shared = jnp.dot(act, ws2_ref[...], preferred_element_type=f32)

    def drain(t, carry):
        for k in range(TOP_K):
            row_copy(0, k, 0).wait()
        return carry

    lax.fori_loop(0, MOE_TM, drain, 0, unroll=2)

    def weighted_sum(t, carry):
        acc = jnp.zeros(TOKEN_TILE, f32)
        for k in range(TOP_K):
            acc = acc + wts_ref[k * n_tok + tok0 + t] * ybuf[k, t].astype(f32)
        r_scr[t] = acc
        return carry

    lax.fori_loop(0, MOE_TM, weighted_sum, 0, unroll=2)
    routed = _load_token_tiles(r_scr)
    o_ref[...] = x1_ref[...] + g2_ref[0] * (routed + shared)


def _combine(pos_flat, wts_flat, ys, h2t, x1, g2, ws1, ws3, ws2, tiles_per_batch):
    n_tok, d = x1.shape
    vec = lambda i: (0, 0)
    smem = pl.BlockSpec(memory_space=pltpu.SMEM)
    return pl.pallas_call(
        _combine_kernel,
        out_shape=jax.ShapeDtypeStruct((n_tok, d), f32),
        grid=(n_tok // MOE_TM,),
        in_specs=[smem, smem,
                  pl.BlockSpec(memory_space=pl.ANY),
                  pl.BlockSpec((MOE_TM,) + TOKEN_TILE, lambda i: (i, 0, 0)),
                  pl.BlockSpec((MOE_TM, d), lambda i: (i, 0)),
                  pl.BlockSpec((1, 1, d), lambda i: (i // tiles_per_batch, 0, 0)),
                  pl.BlockSpec(ws1.shape, vec),
                  pl.BlockSpec(ws3.shape, vec),
                  pl.BlockSpec(ws2.shape, vec)],
        out_specs=pl.BlockSpec((MOE_TM, d), lambda i: (i, 0)),
        scratch_shapes=[pltpu.VMEM((TOP_K, MOE_TM) + TOKEN_TILE, bf16), pltpu.VMEM((MOE_TM,) + TOKEN_TILE, f32),
                        pltpu.SemaphoreType.DMA(())],
        compiler_params=_cparams(("arbitrary",)),
        name="combine",
    )(pos_flat, wts_flat, ys, h2t, x1, g2, ws1, ws3, ws2)


def _layer(x, ctx, mod, norm1_g, norm2_g, w_in, q_norm_g, k_norm_g, na_rpb, up_f, bias_f, up_b, bias_b, gla_norm_g,
           w_out, router_w, router_bias, w1, w3, w2, ws1, ws3, ws2):
    batch, l, d = x.shape
    lc = ctx.shape[1]
    n_tok = batch * l
    sh1, sc1, g1, sh2, sc2, g2 = [mod[:batch, k * d:(k + 1) * d].reshape(batch, 1, d) for k in range(6)]
    csh1, csc1 = [mod[batch:batch + 1, k * d:(k + 1) * d].reshape(1, 1, d) for k in range(2)]

    w_bf = w_in[:, :COL_DD].astype(bf16)
    wdd_bf = jnp.zeros((d, 128), f32).at[:, :2 * GLA_GATE_RANK].set(w_in[:, COL_DD:]).astype(bf16)
    qg, kg = q_norm_g.reshape(1, -1), k_norm_g.reshape(1, -1)
    tabs = _rope_tables(l)
    n1 = norm1_g.reshape(1, d)
    tm_lat = min(512, l)
    p_lat, dd_lat = _inproj(x.reshape(n_tok, d), sc1, sh1, n1, w_bf, wdd_bf, qg, kg, tabs, tm=tm_lat,
                            tiles_per_group=l // tm_lat, rope=True)
    tm_ctx = batch * lc
    ctabs = tuple(t[:tm_ctx] for t in tabs)
    p_ctx, dd_ctx = _inproj(ctx.reshape(batch * lc, d), csc1, csh1, n1, w_bf, wdd_bf, qg, kg, ctabs, tm=tm_ctx,
                            tiles_per_group=1, rope=False)

    bias = _na_bias_tables(na_rpb, l // GRID_W)
    o_na = _na(p_lat, p_ctx, bias, batch, l, lc)

    o_f, o_b = _gla(p_lat, dd_lat, p_ctx, dd_ctx, _pad_gate_up(up_f, 0), _pad_gate_up(up_b, GLA_GATE_RANK),
                    bias_f.reshape(1, -1), bias_b.reshape(1, -1), batch, l, lc)

    tm_out = 256
    x1, h2t, logits_t = _outproj(o_na, o_f, o_b, p_lat, gla_norm_g.reshape(1, -1), w_out.astype(bf16),
                                 x.reshape(n_tok, d), g1, norm2_g.reshape(1, d), sc2, sh2, router_w.T,
                                 l // tm_out, tm_out)

    n_blocks = n_tok * TOP_K // MOE_TM + N_EXPERTS
    pos, wts, meta, blk = _route(logits_t, router_bias, n_blocks)
    pos_flat = pos[:TOP_K].reshape(-1)
    ys = _experts(blk[0], blk[1, :1], pos_flat, meta[0, :, 0], meta[1, :, 0], h2t, w1, w3, w2, n_blocks)
    out = _combine(pos_flat, wts[:TOP_K].reshape(-1), ys, h2t, x1, g2, ws1.astype(bf16), ws3.astype(bf16),
                   ws2.astype(bf16), l // MOE_TM)
    return out.reshape(batch, l, d)


def kernel(x, c, ctx, c_ctx, w_mod, b_mod, norm1_g, norm2_g, w_in, q_norm_g, k_norm_g, na_rpb, gla_gate_up_f,
           gla_gate_bias_f, gla_gate_up_b, gla_gate_bias_b, gla_norm_g, w_out, router_w, router_bias, expert_w1,
           expert_w3, expert_w2, shared_w1, shared_w3, shared_w2):
    depth = w_mod.shape[0]
    assert depth == 1, "context-stream outputs are only dropped for a single (last) layer"
    batch, d = c.shape
    cond8 = jnp.zeros((8, d), f32).at[:batch].set(c).at[batch].set(c_ctx)
    mod = _adaln(cond8, w_mod[0], b_mod[0])
    return _layer(x, ctx, mod, norm1_g[0], norm2_g[0], w_in[0], q_norm_g[0], k_norm_g[0], na_rpb[0],
                  gla_gate_up_f[0], gla_gate_bias_f[0], gla_gate_up_b[0], gla_gate_bias_b[0], gla_norm_g[0],
                  w_out[0], router_w[0], router_bias[0], expert_w1[0], expert_w3[0], expert_w2[0], shared_w1[0],
                  shared_w3[0], shared_w2[0])
```

```python
import functools

import jax
import jax.numpy as jnp
from jax import lax
from jax.experimental import pallas as pl
from jax.experimental.pallas import tpu as pltpu

f32 = jnp.float32
bf16 = jnp.bfloat16
i32 = jnp.int32

GRID_W = 64
NORM_EPS = 1e-6
NA_HEADS = 8
NA_HEAD_DIM = 128
NA_WIN_H = 8
NA_WIN_W = 16
GLA_HEADS = 4
GLA_DK = 128
GLA_DV = 256
GLA_GATE_RANK = 16
GLA_GATE_TAU = 16.0
GLA_CHUNK = 64
GLA_SUB = 16
ROPE_BASE = 10000.0
N_EXPERTS = 64
N_GROUPS = 8
TOPK_GROUPS = 4
TOP_K = 6
D_EXPERT = 512
ROUTED_SCALE = 2.5

NA_WIDTH = NA_HEADS * NA_HEAD_DIM
GLA_KEY_WIDTH = GLA_HEADS * GLA_DK
GLA_VAL_WIDTH = GLA_HEADS * GLA_DV
COL_NQ, COL_NK, COL_NV = 0, NA_WIDTH, 2 * NA_WIDTH
COL_GQ = 3 * NA_WIDTH
COL_GK = COL_GQ + GLA_KEY_WIDTH
COL_GV = COL_GK + GLA_KEY_WIDTH
COL_GATE = COL_GV + GLA_VAL_WIDTH
COL_DD = COL_GATE + GLA_VAL_WIDTH
PROJ_TN = 512
NA_QROWS = 4
NA_KROWS = 12
MOE_TM = 256
COMBINE_TM = 128
NEG = -1e30

VMEM_LIMIT = 56 * 1024 * 1024


def _cparams(sem, vmem=VMEM_LIMIT):
    return pltpu.CompilerParams(dimension_semantics=sem, vmem_limit_bytes=vmem)


NT_DIMS = (((1,), (1,)), ((), ()))
TN_DIMS = (((0,), (0,)), ((), ()))


def _mod_kernel(c_ref, w_ref, b_ref, o_ref):
    c = c_ref[...]
    s = (c * jax.nn.sigmoid(c)).astype(bf16)
    o_ref[...] = jnp.dot(s, w_ref[...].astype(bf16), preferred_element_type=f32) + b_ref[...]


def _adaln(cond8, w_mod, b_mod):
    d, n = w_mod.shape
    tn = 1024
    return pl.pallas_call(
        _mod_kernel,
        out_shape=jax.ShapeDtypeStruct((8, n), f32),
        grid=(n // tn,),
        in_specs=[pl.BlockSpec((8, d), lambda j: (0, 0)),
                  pl.BlockSpec((d, tn), lambda j: (0, j)),
                  pl.BlockSpec((1, tn), lambda j: (0, j))],
        out_specs=pl.BlockSpec((8, tn), lambda j: (0, j)),
        compiler_params=_cparams(("arbitrary",)),
        name="mod",
    )(cond8, w_mod, b_mod.reshape(1, n))


def _head_rms(a, g, width):
    parts = []
    for h in range(a.shape[-1] // width):
        ah = a[:, h * width:(h + 1) * width]
        ms = jnp.mean(ah * ah, axis=-1, keepdims=True)
        parts.append(ah * lax.rsqrt(ms + NORM_EPS) * g)
    return jnp.concatenate(parts, axis=-1)


def _rope(a, cos, sin_lo, sin_hi):
    parts = []
    for h in range(a.shape[-1] // GLA_DK):
        ah = a[:, h * GLA_DK:(h + 1) * GLA_DK]
        parts.append(ah * cos + pltpu.roll(ah, 32, axis=1) * sin_hi + pltpu.roll(ah, 96, axis=1) * sin_lo)
    return jnp.concatenate(parts, axis=-1)


def _inproj_kernel(x_ref, sc_ref, sh_ref, g_ref, w_ref, wdd_ref, qg_ref, kg_ref, cos_ref, slo_ref, shi_ref,
                   o_ref, dd_ref, h_scr, *, rope):
    j = pl.program_id(1)

    @pl.when(j == 0)
    def _():
        x = x_ref[...]
        ms = jnp.mean(x * x, axis=-1, keepdims=True)
        y = x * lax.rsqrt(ms + NORM_EPS) * g_ref[...]
        hb = (y * (1.0 + sc_ref[0]) + sh_ref[0]).astype(bf16)
        h_scr[...] = hb
        dd_ref[...] = jnp.dot(hb, wdd_ref[...], preferred_element_type=f32)

    acc = jnp.dot(h_scr[...], w_ref[...], preferred_element_type=f32)
    jq, jk, jgq, jgk, jgate = (COL_NQ // PROJ_TN, COL_NK // PROJ_TN, COL_GQ // PROJ_TN, COL_GK // PROJ_TN,
                               COL_GATE // PROJ_TN)

    @pl.when(j < jk)
    def _():
        o_ref[...] = _head_rms(acc, qg_ref[...], NA_HEAD_DIM).astype(bf16)

    @pl.when((j >= jk) & (j < COL_NV // PROJ_TN))
    def _():
        o_ref[...] = _head_rms(acc, kg_ref[...], NA_HEAD_DIM).astype(bf16)

    @pl.when(j == jgq)
    def _():
        a = _rope(acc, cos_ref[...], slo_ref[...], shi_ref[...]) if rope else acc
        o_ref[...] = (a * (GLA_DK ** -0.5)).astype(bf16)

    @pl.when(j == jgk)
    def _():
        a = _rope(acc, cos_ref[...], slo_ref[...], shi_ref[...]) if rope else acc
        o_ref[...] = a.astype(bf16)

    @pl.when(j >= jgate)
    def _():
        o_ref[...] = (acc * jax.nn.sigmoid(acc)).astype(bf16)

    @pl.when(((j >= COL_NV // PROJ_TN) & (j < jgq)) | ((j > jgk) & (j < jgate)))
    def _():
        o_ref[...] = acc.astype(bf16)


def _inproj(x2d, sc, sh, norm_g, w_bf, wdd_bf, qg, kg, rope_tabs, *, tm, tiles_per_group, rope):
    m, d = x2d.shape
    n = w_bf.shape[1]
    cos, slo, shi = rope_tabs
    rope_tiles = cos.shape[0] // tm
    grid = (m // tm, n // PROJ_TN)
    vec = lambda i, j: (0, 0)
    return pl.pallas_call(
        functools.partial(_inproj_kernel, rope=rope),
        out_shape=[jax.ShapeDtypeStruct((m, n), bf16), jax.ShapeDtypeStruct((m, 128), f32)],
        grid=grid,
        in_specs=[pl.BlockSpec((tm, d), lambda i, j: (i, 0)),
                  pl.BlockSpec((1, 1, d), lambda i, j: (i // tiles_per_group, 0, 0)),
                  pl.BlockSpec((1, 1, d), lambda i, j: (i // tiles_per_group, 0, 0)),
                  pl.BlockSpec((1, d), vec),
                  pl.BlockSpec((d, PROJ_TN), lambda i, j: (0, j)),
                  pl.BlockSpec((d, 128), vec),
                  pl.BlockSpec((1, NA_HEAD_DIM), vec),
                  pl.BlockSpec((1, NA_HEAD_DIM), vec),
                  pl.BlockSpec((tm, GLA_DK), lambda i, j: (i % rope_tiles, 0)),
                  pl.BlockSpec((tm, GLA_DK), lambda i, j: (i % rope_tiles, 0)),
                  pl.BlockSpec((tm, GLA_DK), lambda i, j: (i % rope_tiles, 0))],
        out_specs=[pl.BlockSpec((tm, PROJ_TN), lambda i, j: (i, j)),
                   pl.BlockSpec((tm, 128), lambda i, j: (i, 0))],
        scratch_shapes=[pltpu.VMEM((tm, d), bf16)],
        compiler_params=_cparams(("arbitrary", "arbitrary")),
        name="inproj_rope" if rope else "inproj_ctx",
    )(x2d, sc, sh, norm_g, w_bf, wdd_bf, qg, kg, cos, slo, shi)


def _rope_tables(l):
    t = jnp.arange(l)
    half = GLA_DK // 4
    inv_freq = ROPE_BASE ** (-jnp.arange(half, dtype=f32) / half)
    ang_r = (t // GRID_W).astype(f32)[:, None] * inv_freq[None, :]
    ang_c = (t % GRID_W).astype(f32)[:, None] * inv_freq[None, :]
    cos = jnp.concatenate([jnp.cos(ang_r)] * 2 + [jnp.cos(ang_c)] * 2, axis=-1)
    sin = jnp.concatenate([jnp.sin(ang_r)] * 2 + [jnp.sin(ang_c)] * 2, axis=-1)
    lane = jnp.arange(GLA_DK) % (2 * half)
    sin_lo = jnp.where(lane[None, :] < half, -sin, 0.0)
    sin_hi = jnp.where(lane[None, :] >= half, sin, 0.0)
    return cos, sin_lo, sin_hi


def _na_kernel(q_ref, k_ref, v_ref, kc_ref, vc_ref, bias_ref, o_ref, *, rows):
    a = pl.program_id(2)
    start = pl.multiple_of(jnp.clip(NA_QROWS * a - NA_WIN_H // 2, 0, rows - NA_KROWS) * GRID_W, GRID_W)
    scale = NA_HEAD_DIM ** -0.5
    q = q_ref[...]
    k = k_ref[pl.ds(start, NA_KROWS * GRID_W), :]
    v = v_ref[pl.ds(start, NA_KROWS * GRID_W), :]
    s_loc = lax.dot_general(q, k, NT_DIMS, preferred_element_type=f32) * scale + bias_ref[0, 0]
    s_ctx = lax.dot_general(q, kc_ref[...], NT_DIMS, preferred_element_type=f32) * scale
    m = jnp.maximum(jnp.max(s_loc, axis=-1, keepdims=True), jnp.max(s_ctx, axis=-1, keepdims=True))
    p_loc = jnp.exp(s_loc - m)
    p_ctx = jnp.exp(s_ctx - m)
    denom = jnp.sum(p_loc, axis=-1, keepdims=True) + jnp.sum(p_ctx, axis=-1, keepdims=True)
    o = (jnp.dot(p_loc.astype(bf16), v, preferred_element_type=f32)
         + jnp.dot(p_ctx.astype(bf16), vc_ref[...], preferred_element_type=f32))
    o_ref[...] = (o / denom).astype(bf16)


def _na_bias_tables(rpb, rows):
    n_h, n_dr, _ = rpb.shape
    n_a = rows // NA_QROWS
    band_rows = []
    for c in range(GRID_W):
        cs = min(max(c - NA_WIN_W // 2, 0), GRID_W - NA_WIN_W)
        d0 = cs - c + NA_WIN_W - 1
        band_rows.append(jnp.pad(rpb[:, :, d0:d0 + NA_WIN_W], ((0, 0), (0, 0), (cs, GRID_W - NA_WIN_W - cs)),
                                 constant_values=NEG))
    band = jnp.stack(band_rows, axis=2)
    masked = jnp.full((n_h, GRID_W, GRID_W), NEG, f32)
    tabs = []
    for a in (0, 1, n_a - 1):
        start = min(max(NA_QROWS * a - NA_WIN_H // 2, 0), rows - NA_KROWS)
        q_blocks = []
        for qr in range(NA_QROWS):
            r_abs = NA_QROWS * a + qr
            rs = min(max(r_abs - NA_WIN_H // 2, 0), rows - NA_WIN_H)
            k_blocks = []
            for kr in range(NA_KROWS):
                k_abs = start + kr
                ok = rs <= k_abs < rs + NA_WIN_H
                k_blocks.append(band[:, k_abs - r_abs + NA_WIN_H - 1] if ok else masked)
            q_blocks.append(jnp.concatenate(k_blocks, axis=-1))
        tabs.append(jnp.concatenate(q_blocks, axis=-2))
    return jnp.stack(tabs, axis=1).astype(f32)


def _na(p_lat, p_ctx, bias, batch, l, lc):
    rows = l // GRID_W
    n_a = rows // NA_QROWS
    tq = NA_QROWS * GRID_W
    hk, hv = COL_NK // NA_HEAD_DIM, COL_NV // NA_HEAD_DIM

    def variant(a):
        return jnp.where(a == 0, 0, jnp.where(a == n_a - 1, 2, 1))

    return pl.pallas_call(
        functools.partial(_na_kernel, rows=rows),
        out_shape=jax.ShapeDtypeStruct((batch * l, NA_WIDTH), bf16),
        grid=(batch, NA_HEADS, n_a),
        in_specs=[pl.BlockSpec((tq, NA_HEAD_DIM), lambda b, h, a: (b * n_a + a, h)),
                  pl.BlockSpec((l, NA_HEAD_DIM), lambda b, h, a: (b, hk + h)),
                  pl.BlockSpec((l, NA_HEAD_DIM), lambda b, h, a: (b, hv + h)),
                  pl.BlockSpec((lc, NA_HEAD_DIM), lambda b, h, a: (b, hk + h)),
                  pl.BlockSpec((lc, NA_HEAD_DIM), lambda b, h, a: (b, hv + h)),
                  pl.BlockSpec((1, 1, tq, NA_KROWS * GRID_W), lambda b, h, a: (h, variant(a), 0, 0))],
        out_specs=pl.BlockSpec((tq, NA_HEAD_DIM), lambda b, h, a: (b * n_a + a, h)),
        compiler_params=_cparams(("arbitrary", "arbitrary", "arbitrary")),
        name="na",
    )(p_lat, p_lat, p_lat, p_ctx, p_ctx, bias)


def _log_decay(dd, up, bias):
    z = jnp.dot(dd.astype(bf16), up, preferred_element_type=f32) + bias
    return (jnp.minimum(z, 0.0) - jnp.log1p(jnp.exp(-jnp.abs(z)))) * (1.0 / GLA_GATE_TAU)


def _gla_chunk(q, k, v, g, st, *, rev, need_out):
    c, sub = GLA_CHUNK, GLA_SUB
    r = lax.broadcasted_iota(i32, (c, c), 0)
    col = lax.broadcasted_iota(i32, (c, c), 1)
    tri = ((r <= col) if rev else (r >= col)).astype(f32)
    cum = jnp.dot(tri, g, precision=lax.Precision.HIGHEST, preferred_element_type=f32)
    total = cum[0:1] if rev else cum[c - 1:c]
    kd = (k * jnp.exp(total - cum)).astype(bf16)
    st_new = st * jnp.exp(total) + lax.dot_general(v, kd, TN_DIMS, preferred_element_type=f32)
    if not need_out:
        return None, st_new
    qd = (q * jnp.exp(cum)).astype(bf16)
    o_inter = lax.dot_general(qd, st.astype(bf16), NT_DIMS, preferred_element_type=f32)
    t_io = lax.broadcasted_iota(i32, (sub, 1), 0)
    outs = []
    for i in range(c // sub):
        lo, hi = i * sub, (i + 1) * sub
        q_i, k_i, cum_i, v_i = q[lo:hi], k[lo:hi], cum[lo:hi], v[lo:hi].astype(f32)
        o_i = jnp.zeros((sub, GLA_DV), f32)
        klo, khi = (hi, c) if rev else (0, lo)
        if khi > klo:
            edge = cum[hi:hi + 1] if rev else cum[lo - 1:lo]
            qe = (q_i * jnp.exp(cum_i - edge)).astype(bf16)
            ke = (k[klo:khi] * jnp.exp(edge - cum[klo:khi])).astype(bf16)
            att = lax.dot_general(qe, ke, NT_DIMS, preferred_element_type=f32)
            o_i = o_i + jnp.dot(att.astype(bf16), v[klo:khi], preferred_element_type=f32)
        for s in range(sub):
            ok = (t_io <= s) if rev else (t_io >= s)
            e = jnp.exp(jnp.where(ok, cum_i - cum_i[s:s + 1], NEG))
            a_col = jnp.sum(q_i * k_i[s:s + 1] * e, axis=-1, keepdims=True)
            o_i = o_i + a_col * v_i[s:s + 1]
        outs.append(o_i)
    return o_inter + jnp.concatenate(outs, axis=0), st_new


def _gla_kernel(qf_ref, kf_ref, vf_ref, ddf_ref, qb_ref, kb_ref, vb_ref, ddb_ref, kc_ref, vc_ref, ddc_ref,
                upf_ref, upb_ref, bf_ref, bb_ref, of_ref, ob_ref, sf_scr, sb_scr):
    j = pl.program_id(2)
    n_chunks = kc_ref.shape[0] // GLA_CHUNK
    upf, upb, bias_f, bias_b = upf_ref[0], upb_ref[0], bf_ref[...], bb_ref[...]

    def rows(ref, ci):
        return ref[ci * GLA_CHUNK:(ci + 1) * GLA_CHUNK, :]

    @pl.when(j == 0)
    def _():
        sf = jnp.zeros((GLA_DV, GLA_DK), f32)
        sb = jnp.zeros((GLA_DV, GLA_DK), f32)
        for ci in range(n_chunks):
            cb = n_chunks - 1 - ci
            gf = _log_decay(rows(ddc_ref, ci), upf, bias_f)
            gb = _log_decay(rows(ddc_ref, cb), upb, bias_b)
            _, sf = _gla_chunk(None, rows(kc_ref, ci).astype(f32), rows(vc_ref, ci), gf, sf, rev=False,
                               need_out=False)
            _, sb = _gla_chunk(None, rows(kc_ref, cb).astype(f32), rows(vc_ref, cb), gb, sb, rev=True,
                               need_out=False)
        sf_scr[...] = sf
        sb_scr[...] = sb

    @pl.when(j > 0)
    def _():
        sf = sf_scr[...]
        sb = sb_scr[...]
        nc = kf_ref.shape[0] // GLA_CHUNK
        for ci in range(nc):
            cb = nc - 1 - ci
            gf = _log_decay(rows(ddf_ref, ci), upf, bias_f)
            gb = _log_decay(rows(ddb_ref, cb), upb, bias_b)
            o_f, sf = _gla_chunk(rows(qf_ref, ci).astype(f32), rows(kf_ref, ci).astype(f32), rows(vf_ref, ci), gf,
                                 sf, rev=False, need_out=True)
            o_b, sb = _gla_chunk(rows(qb_ref, cb).astype(f32), rows(kb_ref, cb).astype(f32), rows(vb_ref, cb), gb,
                                 sb, rev=True, need_out=True)
            of_ref[ci * GLA_CHUNK:(ci + 1) * GLA_CHUNK, :] = o_f.astype(bf16)
            ob_ref[cb * GLA_CHUNK:(cb + 1) * GLA_CHUNK, :] = o_b.astype(bf16)
        sf_scr[...] = sf
        sb_scr[...] = sb


def _gla(p_lat, dd_lat, p_ctx, dd_ctx, upf, upb, bias_f, bias_b, batch, l, lc):
    tt = lc
    n_l = l // tt
    hq, hk, hv = COL_GQ // GLA_DK, COL_GK // GLA_DK, COL_GV // GLA_DV

    def fwd(b, h, j):
        return b * n_l + jnp.maximum(j - 1, 0)

    def bwd(b, h, j):
        return b * n_l + n_l - jnp.maximum(j, 1)

    lat = lambda rowmap, width, col: pl.BlockSpec((tt, width), lambda b, h, j: (rowmap(b, h, j), col + h))
    ddspec = lambda rowmap: pl.BlockSpec((tt, 128), lambda b, h, j: (rowmap(b, h, j), 0))
    ctx = lambda width, col: pl.BlockSpec((tt, width), lambda b, h, j: (b, col + h))
    return pl.pallas_call(
        _gla_kernel,
        out_shape=[jax.ShapeDtypeStruct((batch * l, GLA_VAL_WIDTH), bf16)] * 2,
        grid=(batch, GLA_HEADS, n_l + 1),
        in_specs=[lat(fwd, GLA_DK, hq), lat(fwd, GLA_DK, hk), lat(fwd, GLA_DV, hv), ddspec(fwd),
                  lat(bwd, GLA_DK, hq), lat(bwd, GLA_DK, hk), lat(bwd, GLA_DV, hv), ddspec(bwd),
                  ctx(GLA_DK, hk), ctx(GLA_DV, hv), pl.BlockSpec((tt, 128), lambda b, h, j: (b, 0)),
                  pl.BlockSpec((1, 128, GLA_DK), lambda b, h, j: (h, 0, 0)),
                  pl.BlockSpec((1, 128, GLA_DK), lambda b, h, j: (h, 0, 0)),
                  pl.BlockSpec((1, GLA_DK), lambda b, h, j: (0, h)),
                  pl.BlockSpec((1, GLA_DK), lambda b, h, j: (0, h))],
        out_specs=[pl.BlockSpec((tt, GLA_DV), lambda b, h, j: (fwd(b, h, j), h)),
                   pl.BlockSpec((tt, GLA_DV), lambda b, h, j: (bwd(b, h, j), h))],
        scratch_shapes=[pltpu.VMEM((GLA_DV, GLA_DK), f32), pltpu.VMEM((GLA_DV, GLA_DK), f32)],
        compiler_params=_cparams(("arbitrary", "arbitrary", "arbitrary")),
        name="gla",
    )(p_lat, p_lat, p_lat, dd_lat, p_lat, p_lat, p_lat, dd_lat, p_ctx, p_ctx, dd_ctx, upf, upb, bias_f, bias_b)


def _pad_gate_up(up, row0):
    w = up.reshape(GLA_GATE_RANK, GLA_HEADS, GLA_DK).transpose(1, 0, 2)
    return jnp.zeros((GLA_HEADS, 128, GLA_DK), f32).at[:, row0:row0 + GLA_GATE_RANK].set(w).astype(bf16)


def _outproj_kernel(ona_ref, of_ref, ob_ref, gate_ref, gng_ref, w_ref, x_ref, g1_ref, n2g_ref, sc2_ref, sh2_ref,
                    rwt_ref, x1_ref, h2_ref, lg_ref):
    o = of_ref[...].astype(f32) + ob_ref[...].astype(f32)
    gla = (_head_rms(o, gng_ref[...], GLA_DV) * gate_ref[...].astype(f32)).astype(bf16)
    acc = (jnp.dot(ona_ref[...], w_ref[0:NA_WIDTH, :], preferred_element_type=f32)
           + jnp.dot(gla, w_ref[NA_WIDTH:, :], preferred_element_type=f32))
    x1 = x_ref[...] + g1_ref[0] * acc
    x1_ref[...] = x1
    ms = jnp.mean(x1 * x1, axis=-1, keepdims=True)
    h2 = x1 * lax.rsqrt(ms + NORM_EPS) * n2g_ref[...] * (1.0 + sc2_ref[0]) + sh2_ref[0]
    lg_ref[...] = lax.dot_general(rwt_ref[...], h2, NT_DIMS, precision=lax.Precision.HIGHEST,
                                  preferred_element_type=f32)
    h2_ref[...] = h2


def _outproj(o_na, o_f, o_b, p_lat, gnorm_g, w_out_bf, x2d, g1, norm2_g, sc2, sh2, rwt, tiles_per_batch, tm):
    m, d = x2d.shape
    vec = lambda i: (0, 0)
    per_b = lambda i: (i // tiles_per_batch, 0, 0)
    return pl.pallas_call(
        _outproj_kernel,
        out_shape=[jax.ShapeDtypeStruct((m, d), f32), jax.ShapeDtypeStruct((m, d), f32),
                   jax.ShapeDtypeStruct((N_EXPERTS, m), f32)],
        grid=(m // tm,),
        in_specs=[pl.BlockSpec((tm, NA_WIDTH), lambda i: (i, 0)),
                  pl.BlockSpec((tm, GLA_VAL_WIDTH), lambda i: (i, 0)),
                  pl.BlockSpec((tm, GLA_VAL_WIDTH), lambda i: (i, 0)),
                  pl.BlockSpec((tm, GLA_VAL_WIDTH), lambda i: (i, COL_GATE // GLA_VAL_WIDTH)),
                  pl.BlockSpec((1, GLA_DV), vec),
                  pl.BlockSpec(w_out_bf.shape, vec),
                  pl.BlockSpec((tm, d), lambda i: (i, 0)),
                  pl.BlockSpec((1, 1, d), per_b),
                  pl.BlockSpec((1, d), vec),
                  pl.BlockSpec((1, 1, d), per_b),
                  pl.BlockSpec((1, 1, d), per_b),
                  pl.BlockSpec((N_EXPERTS, d), vec)],
        out_specs=[pl.BlockSpec((tm, d), lambda i: (i, 0)),
                   pl.BlockSpec((tm, d), lambda i: (i, 0)),
                   pl.BlockSpec((N_EXPERTS, tm), lambda i: (0, i))],
        compiler_params=_cparams(("arbitrary",)),
        name="outproj",
    )(o_na, o_f, o_b, p_lat, gnorm_g, w_out_bf, x2d, g1, norm2_g, sc2, sh2, rwt)


def _first_argmax(vals, iota, axis, sentinel):
    mx = jnp.max(vals, axis=axis, keepdims=True)
    am = jnp.min(jnp.where(vals == mx, iota, sentinel), axis=axis, keepdims=True)
    return mx, am


def _route_tile(lg, rb):
    t = lg.shape[1]
    per = N_EXPERTS // N_GROUPS
    scores = jax.nn.sigmoid(lg)
    biased = scores + rb
    b3 = biased.reshape(N_GROUPS, per, t)
    io3 = lax.broadcasted_iota(i32, (N_GROUPS, per, t), 1)
    m1, a1 = _first_argmax(b3, io3, 1, per)
    m2 = jnp.max(jnp.where(io3 == a1, -jnp.inf, b3), axis=1, keepdims=True)
    gs = (m1 + m2).reshape(N_GROUPS, t)
    iog = lax.broadcasted_iota(i32, (N_GROUPS, t), 0)
    gsel = jnp.zeros((N_GROUPS, t), jnp.bool_)
    for _ in range(TOPK_GROUPS):
        _, am = _first_argmax(gs, iog, 0, N_GROUPS)
        hit = iog == am
        gsel = gsel | hit
        gs = jnp.where(hit, -jnp.inf, gs)
    emask = jnp.broadcast_to(gsel[:, None, :], (N_GROUPS, per, t)).reshape(N_EXPERTS, t)
    masked = jnp.where(emask, biased, -jnp.inf)
    ioe = lax.broadcasted_iota(i32, (N_EXPERTS, t), 0)
    hits, sels = [], []
    for _ in range(TOP_K):
        _, am = _first_argmax(masked, ioe, 0, N_EXPERTS)
        hit = ioe == am
        hits.append(hit)
        sels.append(jnp.sum(jnp.where(hit, scores, 0.0), axis=0, keepdims=True))
        masked = jnp.where(hit, -jnp.inf, masked)
    denom = sels[0]
    for s in sels[1:]:
        denom = denom + s
    wts = [s / denom * ROUTED_SCALE for s in sels]
    return hits, wts


def _route_kernel(lg_ref, rb_ref, pos_ref, wts_ref, meta_ref, blk_ref, cnt_scr, carry_scr, base_scr, tri_scr):
    ph = pl.program_id(0)
    i = pl.program_id(1)
    t = lg_ref.shape[1]
    hits, wts = _route_tile(lg_ref[...], rb_ref[...])
    sel = hits[0]
    for h in hits[1:]:
        sel = sel | h
    self32 = sel.astype(f32)
    tile_cnt = jnp.sum(self32, axis=1, keepdims=True)

    @pl.when((ph == 0) & (i == 0))
    def _():
        cnt_scr[...] = jnp.zeros_like(cnt_scr)

    @pl.when(ph == 0)
    def _():
        cnt_scr[...] += jnp.broadcast_to(tile_cnt, cnt_scr.shape)

    @pl.when((ph == 1) & (i == 0))
    def _():
        cnt = cnt_scr[...]
        nblk = jnp.floor((cnt + (MOE_TM - 1)) * (1.0 / MOE_TM))
        r = lax.broadcasted_iota(i32, (N_EXPERTS, N_EXPERTS), 0)
        c = lax.broadcasted_iota(i32, (N_EXPERTS, N_EXPERTS), 1)
        excl = jnp.dot((c < r).astype(bf16), nblk.astype(bf16), preferred_element_type=f32)
        base_scr[...] = excl * MOE_TM
        carry_scr[...] = jnp.zeros_like(carry_scr)
        meta_ref[0] = cnt.astype(i32)
        meta_ref[1] = (excl * MOE_TM).astype(i32)
        incl = excl + nblk
        nb = blk_ref.shape[1]
        jj = lax.broadcasted_iota(i32, (N_EXPERTS, nb), 1).astype(f32)
        owner = jnp.sum((jnp.broadcast_to(incl[:, 0:1], (N_EXPERTS, nb)) <= jj).astype(i32), axis=0, keepdims=True)
        blk_ref[0:1, :] = jnp.minimum(owner, N_EXPERTS - 1)
        blk_ref[1:2, :] = jnp.broadcast_to(incl[N_EXPERTS - 1:N_EXPERTS, 0:1].astype(i32), (1, nb))
        blk_ref[2:8, :] = jnp.zeros((6, nb), i32)
        rr = lax.broadcasted_iota(i32, (t, t), 0)
        cc = lax.broadcasted_iota(i32, (t, t), 1)
        tri_scr[...] = (rr < cc).astype(bf16)

    @pl.when(ph == 1)
    def _():
        rank = jnp.dot(sel.astype(bf16), tri_scr[...], preferred_element_type=f32)
        posf = base_scr[:, 0:1] + carry_scr[:, 0:1] + rank
        for k in range(TOP_K):
            pos_ref[k:k + 1, :] = jnp.sum(jnp.where(hits[k], posf, 0.0), axis=0, keepdims=True).astype(i32)
            wts_ref[k:k + 1, :] = wts[k]
        pos_ref[TOP_K:8, :] = jnp.zeros((8 - TOP_K, t), i32)
        wts_ref[TOP_K:8, :] = jnp.zeros((8 - TOP_K, t), f32)
        carry_scr[...] += jnp.broadcast_to(tile_cnt, carry_scr.shape)


def _route(logits_t, router_bias, n_blocks):
    n_tok = logits_t.shape[1]
    tt = 1024
    n_tiles = n_tok // tt
    return pl.pallas_call(
        _route_kernel,
        out_shape=[jax.ShapeDtypeStruct((8, n_tok), i32), jax.ShapeDtypeStruct((8, n_tok), f32),
                   jax.ShapeDtypeStruct((2, N_EXPERTS, 128), i32), jax.ShapeDtypeStruct((8, n_blocks), i32)],
        grid=(2, n_tiles),
        in_specs=[pl.BlockSpec((N_EXPERTS, tt), lambda p, i: (0, i)),
                  pl.BlockSpec((N_EXPERTS, 1), lambda p, i: (0, 0))],
        out_specs=[pl.BlockSpec((8, tt), lambda p, i: (0, i * p)),
                   pl.BlockSpec((8, tt), lambda p, i: (0, i * p)),
                   pl.BlockSpec((2, N_EXPERTS, 128), lambda p, i: (0, 0, 0)),
                   pl.BlockSpec((8, n_blocks), lambda p, i: (0, 0))],
        scratch_shapes=[pltpu.VMEM((N_EXPERTS, 128), f32), pltpu.VMEM((N_EXPERTS, 128), f32),
                        pltpu.VMEM((N_EXPERTS, 128), f32), pltpu.VMEM((tt, tt), bf16)],
        compiler_params=_cparams(("arbitrary", "arbitrary")),
        name="route",
    )(logits_t, router_bias.reshape(N_EXPERTS, 1))


def _experts_kernel(blk_e_ref, nblk_ref, pos_ref, cnt_ref, base_ref, h2_ref, w1_ref, w3_ref, w2_ref, ys_ref,
                    tok_smem, xbuf, sems, w1_scr, w3_scr, w2_scr):
    j = pl.program_id(0)
    n_valid = nblk_ref[0]
    n_tok = pos_ref.shape[0] // TOP_K

    def row_copy(row, slot, r):
        return pltpu.make_async_copy(h2_ref.at[pl.ds(tok_smem[row], 1)], xbuf.at[slot, pl.ds(r, 1)],
                                     sems.at[slot])

    def start_gather(block, slot):
        def issue(r, carry):
            row_copy(block * MOE_TM + r, slot, r).start()
            return carry

        lax.fori_loop(0, MOE_TM, issue, 0, unroll=8)

    def drain_gather(slot):
        def drain(r, carry):
            row_copy(0, slot, 0).wait()
            return carry

        lax.fori_loop(0, MOE_TM, drain, 0, unroll=8)

    @pl.when(j == 0)
    def _():
        def pad_expert(e, carry):
            cnt = cnt_ref[e]
            first = base_ref[e] + cnt

            def fill(r, c):
                tok_smem[first + r] = 0
                return c

            lax.fori_loop(0, (MOE_TM - cnt % MOE_TM) % MOE_TM, fill, 0)
            return carry

        lax.fori_loop(0, N_EXPERTS, pad_expert, 0)

        def invert(t, carry):
            for k in range(TOP_K):
                tok_smem[pos_ref[k * n_tok + t]] = t
            return carry

        lax.fori_loop(0, n_tok, invert, 0, unroll=4)
        start_gather(0, 0)

    e = blk_e_ref[jnp.minimum(j, n_valid - 1)]
    e_prev = blk_e_ref[jnp.maximum(j - 1, 0)]

    @pl.when((j == 0) | ((j < n_valid) & (e != e_prev)))
    def _():
        w1_scr[...] = w1_ref[0].astype(bf16)
        w3_scr[...] = w3_ref[0].astype(bf16)
        w2_scr[...] = w2_ref[0].astype(bf16)

    @pl.when(j < n_valid)
    def _():
        slot = j % 2
        drain_gather(slot)
        nxt = jnp.minimum(j + 1, n_valid - 1) * MOE_TM
        for r in range(MOE_TM):
            row_copy(nxt + r, 1 - slot, r).start(priority=r % 2)
        x = xbuf[slot].astype(bf16)
        h1 = jnp.dot(x, w1_scr[...], preferred_element_type=f32)
        h3 = jnp.dot(x, w3_scr[...], preferred_element_type=f32)
        act = (h1 * jax.nn.sigmoid(h1) * h3).astype(bf16)
        ys_ref[...] = jnp.dot(act, w2_scr[...], preferred_element_type=f32)

    @pl.when(j == n_valid - 1)
    def _():
        drain_gather(1 - j % 2)

    @pl.when(j >= n_valid)
    def _():
        ys_ref[...] = jnp.zeros_like(ys_ref)


def _experts(blk_expert, nblk, pos_flat, cnt, base, h2, w1, w3, w2, n_blocks):
    d, de = w1.shape[1], w1.shape[2]
    n_rows = n_blocks * MOE_TM

    def w_blk(j, be, nb, *_):
        return (be[jnp.minimum(j, nb[0] - 1)], 0, 0)

    return pl.pallas_call(
        _experts_kernel,
        out_shape=jax.ShapeDtypeStruct((n_rows, d), f32),
        grid_spec=pltpu.PrefetchScalarGridSpec(
            num_scalar_prefetch=5,
            grid=(n_blocks,),
            in_specs=[pl.BlockSpec(memory_space=pl.ANY),
                      pl.BlockSpec((1, d, de), w_blk),
                      pl.BlockSpec((1, d, de), w_blk),
                      pl.BlockSpec((1, de, d), w_blk)],
            out_specs=pl.BlockSpec((MOE_TM, d), lambda j, *_: (j, 0)),
            scratch_shapes=[pltpu.SMEM((n_rows,), i32), pltpu.VMEM((2, MOE_TM, d), f32),
                            pltpu.SemaphoreType.DMA((2,)),
                            pltpu.VMEM((d, de), bf16), pltpu.VMEM((d, de), bf16), pltpu.VMEM((de, d), bf16)]),
        compiler_params=_cparams(("arbitrary",)),
        name="experts",
    )(blk_expert, nblk, pos_flat, cnt, base, h2, w1, w3, w2)


def _combine_kernel(pos_ref, ys_ref, h2_ref, wts_ref, x1_ref, g2_ref, ws1_ref, ws3_ref, ws2_ref, o_ref,
                    ybuf, sems):
    i = pl.program_id(0)
    n_tiles = 2 * pl.num_programs(0)
    n_tok = pos_ref.shape[0] // TOP_K
    tm = COMBINE_TM

    def row_copy(src_row, slot, k, t):
        return pltpu.make_async_copy(ys_ref.at[pl.ds(src_row, 1)], ybuf.at[slot, k, pl.ds(t, 1)], sems.at[slot])

    def drain_gather(slot):
        def drain(t, carry):
            for k in range(TOP_K):
                row_copy(0, slot, k, 0).wait()
            return carry

        lax.fori_loop(0, tm, drain, 0, unroll=2)

    def half_step(slot, next_tile):
        drain_gather(slot)
        for t in range(tm):
            for k in range(TOP_K):
                row_copy(pos_ref[k * n_tok + next_tile * tm + t], 1 - slot, k, t).start(priority=(t + k) % 2)
        rows = slice(slot * tm, (slot + 1) * tm)
        x = h2_ref[rows, :].astype(bf16)
        s1 = jnp.dot(x, ws1_ref[...], preferred_element_type=f32)
        s3 = jnp.dot(x, ws3_ref[...], preferred_element_type=f32)
        act = (s1 * jax.nn.sigmoid(s1) * s3).astype(bf16)
        acc = jnp.dot(act, ws2_ref[...], preferred_element_type=f32)
        w = wts_ref[rows, :]
        for k in range(TOP_K):
            acc = acc + w[:, k:k + 1] * ybuf[slot, k]
        o_ref[rows, :] = x1_ref[rows, :] + g2_ref[0] * acc

    @pl.when(i == 0)
    def _():
        def issue(t, carry):
            for k in range(TOP_K):
                row_copy(pos_ref[k * n_tok + t], 0, k, t).start()
            return carry

        lax.fori_loop(0, tm, issue, 0, unroll=2)

    half_step(0, 2 * i + 1)
    half_step(1, jnp.minimum(2 * i + 2, n_tiles - 1))

    @pl.when(i == pl.num_programs(0) - 1)
    def _():
        drain_gather(0)


def _combine(pos_flat, ys, h2, wts_t, x1, g2, ws1, ws3, ws2, tokens_per_batch):
    n_tok, d = x1.shape
    tm = 2 * COMBINE_TM
    vec = lambda i: (0, 0)
    return pl.pallas_call(
        _combine_kernel,
        out_shape=jax.ShapeDtypeStruct((n_tok, d), f32),
        grid=(n_tok // tm,),
        in_specs=[pl.BlockSpec(memory_space=pltpu.SMEM),
                  pl.BlockSpec(memory_space=pl.ANY),
                  pl.BlockSpec((tm, d), lambda i: (i, 0)),
                  pl.BlockSpec((tm, 8), lambda i: (i, 0)),
                  pl.BlockSpec((tm, d), lambda i: (i, 0)),
                  pl.BlockSpec((1, 1, d), lambda i: (i // (tokens_per_batch // tm), 0, 0)),
                  pl.BlockSpec(ws1.shape, vec),
                  pl.BlockSpec(ws3.shape, vec),
                  pl.BlockSpec(ws2.shape, vec)],
        out_specs=pl.BlockSpec((tm, d), lambda i: (i, 0)),
        scratch_shapes=[pltpu.VMEM((2, TOP_K, COMBINE_TM, d), f32), pltpu.SemaphoreType.DMA((2,))],
        compiler_params=_cparams(("arbitrary",)),
        name="combine",
    )(pos_flat, ys, h2, wts_t, x1, g2, ws1, ws3, ws2)


def _layer(x, ctx, mod, norm1_g, norm2_g, w_in, q_norm_g, k_norm_g, na_rpb, up_f, bias_f, up_b, bias_b, gla_norm_g,
           w_out, router_w, router_bias, w1, w3, w2, ws1, ws3, ws2):
    batch, l, d = x.shape
    lc = ctx.shape[1]
    n_tok = batch * l
    sh1, sc1, g1, sh2, sc2, g2 = [mod[:batch, k * d:(k + 1) * d].reshape(batch, 1, d) for k in range(6)]
    csh1, csc1 = [mod[batch:batch + 1, k * d:(k + 1) * d].reshape(1, 1, d) for k in range(2)]

    w_bf = w_in[:, :COL_DD].astype(bf16)
    wdd_bf = jnp.zeros((d, 128), f32).at[:, :2 * GLA_GATE_RANK].set(w_in[:, COL_DD:]).astype(bf16)
    qg, kg = q_norm_g.reshape(1, -1), k_norm_g.reshape(1, -1)
    tabs = _rope_tables(l)
    n1 = norm1_g.reshape(1, d)
    tm_lat = min(512, l)
    p_lat, dd_lat = _inproj(x.reshape(n_tok, d), sc1, sh1, n1, w_bf, wdd_bf, qg, kg, tabs, tm=tm_lat,
                            tiles_per_group=l // tm_lat, rope=True)
    tm_ctx = batch * lc
    ctabs = tuple(t[:tm_ctx] for t in tabs)
    p_ctx, dd_ctx = _inproj(ctx.reshape(batch * lc, d), csc1, csh1, n1, w_bf, wdd_bf, qg, kg, ctabs, tm=tm_ctx,
                            tiles_per_group=1, rope=False)

    bias = _na_bias_tables(na_rpb, l // GRID_W)
    o_na = _na(p_lat, p_ctx, bias, batch, l, lc)

    o_f, o_b = _gla(p_lat, dd_lat, p_ctx, dd_ctx, _pad_gate_up(up_f, 0), _pad_gate_up(up_b, GLA_GATE_RANK),
                    bias_f.reshape(1, -1), bias_b.reshape(1, -1), batch, l, lc)

    tm_out = 256
    x1, h2, logits_t = _outproj(o_na, o_f, o_b, p_lat, gla_norm_g.reshape(1, -1), w_out.astype(bf16),
                                 x.reshape(n_tok, d), g1, norm2_g.reshape(1, d), sc2, sh2, router_w.T,
                                 l // tm_out, tm_out)

    n_blocks = n_tok * TOP_K // MOE_TM + N_EXPERTS
    pos, wts, meta, blk = _route(logits_t, router_bias, n_blocks)
    pos_flat = pos[:TOP_K].reshape(-1)
    ys = _experts(blk[0], blk[1, :1], pos_flat, meta[0, :, 0], meta[1, :, 0], h2, w1, w3, w2, n_blocks)
    out = _combine(pos_flat, ys, h2, wts.T, x1, g2, ws1.astype(bf16), ws3.astype(bf16), ws2.astype(bf16), l)
    return out.reshape(batch, l, d)


def kernel(x, c, ctx, c_ctx, w_mod, b_mod, norm1_g, norm2_g, w_in, q_norm_g, k_norm_g, na_rpb, gla_gate_up_f,
           gla_gate_bias_f, gla_gate_up_b, gla_gate_bias_b, gla_norm_g, w_out, router_w, router_bias, expert_w1,
           expert_w3, expert_w2, shared_w1, shared_w3, shared_w2):
    depth = w_mod.shape[0]
    assert depth == 1, "context-stream outputs are only dropped for a single (last) layer"
    batch, d = c.shape
    cond8 = jnp.zeros((8, d), f32).at[:batch].set(c).at[batch].set(c_ctx)
    mod = _adaln(cond8, w_mod[0], b_mod[0])
    return _layer(x, ctx, mod, norm1_g[0], norm2_g[0], w_in[0], q_norm_g[0], k_norm_g[0], na_rpb[0],
                  gla_gate_up_f[0], gla_gate_bias_f[0], gla_gate_up_b[0], gla_gate_bias_b[0], gla_norm_g[0],
                  w_out[0], router_w[0], router_bias[0], expert_w1[0], expert_w3[0], expert_w2[0], shared_w1[0],
                  shared_w3[0], shared_w2[0])
```

```python
import functools

import jax
import jax.numpy as jnp
from jax import lax
from jax.experimental import pallas as pl
from jax.experimental.pallas import tpu as pltpu

f32 = jnp.float32
bf16 = jnp.bfloat16
i32 = jnp.int32

GRID_W = 64
NORM_EPS = 1e-6
NA_HEADS = 8
NA_HEAD_DIM = 128
NA_WIN_H = 8
NA_WIN_W = 16
GLA_HEADS = 4
GLA_DK = 128
GLA_DV = 256
GLA_GATE_RANK = 16
GLA_GATE_TAU = 16.0
GLA_CHUNK = 64
GLA_SUB = 16
ROPE_BASE = 10000.0
N_EXPERTS = 64
N_GROUPS = 8
TOPK_GROUPS = 4
TOP_K = 6
D_EXPERT = 512
ROUTED_SCALE = 2.5

NA_WIDTH = NA_HEADS * NA_HEAD_DIM
GLA_KEY_WIDTH = GLA_HEADS * GLA_DK
GLA_VAL_WIDTH = GLA_HEADS * GLA_DV
COL_NQ, COL_NK, COL_NV = 0, NA_WIDTH, 2 * NA_WIDTH
COL_GQ = 3 * NA_WIDTH
COL_GK = COL_GQ + GLA_KEY_WIDTH
COL_GV = COL_GK + GLA_KEY_WIDTH
COL_GATE = COL_GV + GLA_VAL_WIDTH
COL_DD = COL_GATE + GLA_VAL_WIDTH
PROJ_TN = 512
NA_QROWS = 4
NA_KROWS = 12
MOE_TM = 256
COMBINE_TM = 128
GATHER_AHEAD = 2
NEG = -1e30

VMEM_LIMIT = 56 * 1024 * 1024


def _cparams(sem, vmem=VMEM_LIMIT):
    return pltpu.CompilerParams(dimension_semantics=sem, vmem_limit_bytes=vmem)


NT_DIMS = (((1,), (1,)), ((), ()))
TN_DIMS = (((0,), (0,)), ((), ()))


def _mod_kernel(c_ref, w_ref, b_ref, o_ref):
    c = c_ref[...]
    s = (c * jax.nn.sigmoid(c)).astype(bf16)
    o_ref[...] = jnp.dot(s, w_ref[...].astype(bf16), preferred_element_type=f32) + b_ref[...]


def _adaln(cond8, w_mod, b_mod):
    d, n = w_mod.shape
    tn = 1024
    return pl.pallas_call(
        _mod_kernel,
        out_shape=jax.ShapeDtypeStruct((8, n), f32),
        grid=(n // tn,),
        in_specs=[pl.BlockSpec((8, d), lambda j: (0, 0)),
                  pl.BlockSpec((d, tn), lambda j: (0, j)),
                  pl.BlockSpec((1, tn), lambda j: (0, j))],
        out_specs=pl.BlockSpec((8, tn), lambda j: (0, j)),
        compiler_params=_cparams(("arbitrary",)),
        name="mod",
    )(cond8, w_mod, b_mod.reshape(1, n))


def _head_rms(a, g, width):
    parts = []
    for h in range(a.shape[-1] // width):
        ah = a[:, h * width:(h + 1) * width]
        ms = jnp.mean(ah * ah, axis=-1, keepdims=True)
        parts.append(ah * lax.rsqrt(ms + NORM_EPS) * g)
    return jnp.concatenate(parts, axis=-1)


def _rope(a, cos, sin_lo, sin_hi):
    parts = []
    for h in range(a.shape[-1] // GLA_DK):
        ah = a[:, h * GLA_DK:(h + 1) * GLA_DK]
        parts.append(ah * cos + pltpu.roll(ah, 32, axis=1) * sin_hi + pltpu.roll(ah, 96, axis=1) * sin_lo)
    return jnp.concatenate(parts, axis=-1)


def _inproj_kernel(x_ref, sc_ref, sh_ref, g_ref, w_ref, wdd_ref, qg_ref, kg_ref, cos_ref, slo_ref, shi_ref,
                   o_ref, dd_ref, h_scr, *, rope):
    j = pl.program_id(1)

    @pl.when(j == 0)
    def _():
        x = x_ref[...]
        ms = jnp.mean(x * x, axis=-1, keepdims=True)
        y = x * lax.rsqrt(ms + NORM_EPS) * g_ref[...]
        hb = (y * (1.0 + sc_ref[0]) + sh_ref[0]).astype(bf16)
        h_scr[...] = hb
        dd_ref[...] = jnp.dot(hb, wdd_ref[...], preferred_element_type=f32)

    acc = jnp.dot(h_scr[...], w_ref[...], preferred_element_type=f32)
    jq, jk, jgq, jgk, jgate = (COL_NQ // PROJ_TN, COL_NK // PROJ_TN, COL_GQ // PROJ_TN, COL_GK // PROJ_TN,
                               COL_GATE // PROJ_TN)

    @pl.when(j < jk)
    def _():
        o_ref[...] = _head_rms(acc, qg_ref[...], NA_HEAD_DIM).astype(bf16)

    @pl.when((j >= jk) & (j < COL_NV // PROJ_TN))
    def _():
        o_ref[...] = _head_rms(acc, kg_ref[...], NA_HEAD_DIM).astype(bf16)

    @pl.when(j == jgq)
    def _():
        a = _rope(acc, cos_ref[...], slo_ref[...], shi_ref[...]) if rope else acc
        o_ref[...] = (a * (GLA_DK ** -0.5)).astype(bf16)

    @pl.when(j == jgk)
    def _():
        a = _rope(acc, cos_ref[...], slo_ref[...], shi_ref[...]) if rope else acc
        o_ref[...] = a.astype(bf16)

    @pl.when(j >= jgate)
    def _():
        o_ref[...] = (acc * jax.nn.sigmoid(acc)).astype(bf16)

    @pl.when(((j >= COL_NV // PROJ_TN) & (j < jgq)) | ((j > jgk) & (j < jgate)))
    def _():
        o_ref[...] = acc.astype(bf16)


def _inproj(x2d, sc, sh, norm_g, w_bf, wdd_bf, qg, kg, rope_tabs, *, tm, tiles_per_group, rope):
    m, d = x2d.shape
    n = w_bf.shape[1]
    cos, slo, shi = rope_tabs
    rope_tiles = cos.shape[0] // tm
    grid = (m // tm, n // PROJ_TN)
    vec = lambda i, j: (0, 0)
    return pl.pallas_call(
        functools.partial(_inproj_kernel, rope=rope),
        out_shape=[jax.ShapeDtypeStruct((m, n), bf16), jax.ShapeDtypeStruct((m, 128), f32)],
        grid=grid,
        in_specs=[pl.BlockSpec((tm, d), lambda i, j: (i, 0)),
                  pl.BlockSpec((1, 1, d), lambda i, j: (i // tiles_per_group, 0, 0)),
                  pl.BlockSpec((1, 1, d), lambda i, j: (i // tiles_per_group, 0, 0)),
                  pl.BlockSpec((1, d), vec),
                  pl.BlockSpec((d, PROJ_TN), lambda i, j: (0, j)),
                  pl.BlockSpec((d, 128), vec),
                  pl.BlockSpec((1, NA_HEAD_DIM), vec),
                  pl.BlockSpec((1, NA_HEAD_DIM), vec),
                  pl.BlockSpec((tm, GLA_DK), lambda i, j: (i % rope_tiles, 0)),
                  pl.BlockSpec((tm, GLA_DK), lambda i, j: (i % rope_tiles, 0)),
                  pl.BlockSpec((tm, GLA_DK), lambda i, j: (i % rope_tiles, 0))],
        out_specs=[pl.BlockSpec((tm, PROJ_TN), lambda i, j: (i, j)),
                   pl.BlockSpec((tm, 128), lambda i, j: (i, 0))],
        scratch_shapes=[pltpu.VMEM((tm, d), bf16)],
        compiler_params=_cparams(("arbitrary", "arbitrary")),
        name="inproj_rope" if rope else "inproj_ctx",
    )(x2d, sc, sh, norm_g, w_bf, wdd_bf, qg, kg, cos, slo, shi)


def _rope_tables(l):
    t = jnp.arange(l)
    half = GLA_DK // 4
    inv_freq = ROPE_BASE ** (-jnp.arange(half, dtype=f32) / half)
    ang_r = (t // GRID_W).astype(f32)[:, None] * inv_freq[None, :]
    ang_c = (t % GRID_W).astype(f32)[:, None] * inv_freq[None, :]
    cos = jnp.concatenate([jnp.cos(ang_r)] * 2 + [jnp.cos(ang_c)] * 2, axis=-1)
    sin = jnp.concatenate([jnp.sin(ang_r)] * 2 + [jnp.sin(ang_c)] * 2, axis=-1)
    lane = jnp.arange(GLA_DK) % (2 * half)
    sin_lo = jnp.where(lane[None, :] < half, -sin, 0.0)
    sin_hi = jnp.where(lane[None, :] >= half, sin, 0.0)
    return cos, sin_lo, sin_hi


def _na_kernel(q_ref, k_ref, v_ref, kc_ref, vc_ref, bias_ref, o_ref, *, rows):
    a = pl.program_id(2)
    start = pl.multiple_of(jnp.clip(NA_QROWS * a - NA_WIN_H // 2, 0, rows - NA_KROWS) * GRID_W, GRID_W)
    scale = NA_HEAD_DIM ** -0.5
    q = q_ref[...]
    k = k_ref[pl.ds(start, NA_KROWS * GRID_W), :]
    v = v_ref[pl.ds(start, NA_KROWS * GRID_W), :]
    s_loc = lax.dot_general(q, k, NT_DIMS, preferred_element_type=f32) * scale + bias_ref[0, 0]
    s_ctx = lax.dot_general(q, kc_ref[...], NT_DIMS, preferred_element_type=f32) * scale
    m = jnp.maximum(jnp.max(s_loc, axis=-1, keepdims=True), jnp.max(s_ctx, axis=-1, keepdims=True))
    p_loc = jnp.exp(s_loc - m)
    p_ctx = jnp.exp(s_ctx - m)
    denom = jnp.sum(p_loc, axis=-1, keepdims=True) + jnp.sum(p_ctx, axis=-1, keepdims=True)
    o = (jnp.dot(p_loc.astype(bf16), v, preferred_element_type=f32)
         + jnp.dot(p_ctx.astype(bf16), vc_ref[...], preferred_element_type=f32))
    o_ref[...] = (o / denom).astype(bf16)


def _na_bias_tables(rpb, rows):
    n_h, n_dr, _ = rpb.shape
    n_a = rows // NA_QROWS
    band_rows = []
    for c in range(GRID_W):
        cs = min(max(c - NA_WIN_W // 2, 0), GRID_W - NA_WIN_W)
        d0 = cs - c + NA_WIN_W - 1
        band_rows.append(jnp.pad(rpb[:, :, d0:d0 + NA_WIN_W], ((0, 0), (0, 0), (cs, GRID_W - NA_WIN_W - cs)),
                                 constant_values=NEG))
    band = jnp.stack(band_rows, axis=2)
    masked = jnp.full((n_h, GRID_W, GRID_W), NEG, f32)
    tabs = []
    for a in (0, 1, n_a - 1):
        start = min(max(NA_QROWS * a - NA_WIN_H // 2, 0), rows - NA_KROWS)
        q_blocks = []
        for qr in range(NA_QROWS):
            r_abs = NA_QROWS * a + qr
            rs = min(max(r_abs - NA_WIN_H // 2, 0), rows - NA_WIN_H)
            k_blocks = []
            for kr in range(NA_KROWS):
                k_abs = start + kr
                ok = rs <= k_abs < rs + NA_WIN_H
                k_blocks.append(band[:, k_abs - r_abs + NA_WIN_H - 1] if ok else masked)
            q_blocks.append(jnp.concatenate(k_blocks, axis=-1))
        tabs.append(jnp.concatenate(q_blocks, axis=-2))
    return jnp.stack(tabs, axis=1).astype(f32)


def _na(p_lat, p_ctx, bias, batch, l, lc):
    rows = l // GRID_W
    n_a = rows // NA_QROWS
    tq = NA_QROWS * GRID_W
    hk, hv = COL_NK // NA_HEAD_DIM, COL_NV // NA_HEAD_DIM

    def variant(a):
        return jnp.where(a == 0, 0, jnp.where(a == n_a - 1, 2, 1))

    return pl.pallas_call(
        functools.partial(_na_kernel, rows=rows),
        out_shape=jax.ShapeDtypeStruct((batch * l, NA_WIDTH), bf16),
        grid=(batch, NA_HEADS, n_a),
        in_specs=[pl.BlockSpec((tq, NA_HEAD_DIM), lambda b, h, a: (b * n_a + a, h)),
                  pl.BlockSpec((l, NA_HEAD_DIM), lambda b, h, a: (b, hk + h)),
                  pl.BlockSpec((l, NA_HEAD_DIM), lambda b, h, a: (b, hv + h)),
                  pl.BlockSpec((lc, NA_HEAD_DIM), lambda b, h, a: (b, hk + h)),
                  pl.BlockSpec((lc, NA_HEAD_DIM), lambda b, h, a: (b, hv + h)),
                  pl.BlockSpec((1, 1, tq, NA_KROWS * GRID_W), lambda b, h, a: (h, variant(a), 0, 0))],
        out_specs=pl.BlockSpec((tq, NA_HEAD_DIM), lambda b, h, a: (b * n_a + a, h)),
        compiler_params=_cparams(("arbitrary", "arbitrary", "arbitrary")),
        name="na",
    )(p_lat, p_lat, p_lat, p_ctx, p_ctx, bias)


def _log_decay(dd, up, bias):
    z = jnp.dot(dd.astype(bf16), up, preferred_element_type=f32) + bias
    return (jnp.minimum(z, 0.0) - jnp.log1p(jnp.exp(-jnp.abs(z)))) * (1.0 / GLA_GATE_TAU)


def _gla_chunk(q, k, v, g, st, *, rev, need_out):
    c, sub = GLA_CHUNK, GLA_SUB
    r = lax.broadcasted_iota(i32, (c, c), 0)
    col = lax.broadcasted_iota(i32, (c, c), 1)
    tri = ((r <= col) if rev else (r >= col)).astype(f32)
    cum = jnp.dot(tri, g, precision=lax.Precision.HIGHEST, preferred_element_type=f32)
    total = cum[0:1] if rev else cum[c - 1:c]
    kd = (k * jnp.exp(total - cum)).astype(bf16)
    st_new = st * jnp.exp(total) + lax.dot_general(v, kd, TN_DIMS, preferred_element_type=f32)
    if not need_out:
        return None, st_new
    qd = (q * jnp.exp(cum)).astype(bf16)
    o_inter = lax.dot_general(qd, st.astype(bf16), NT_DIMS, preferred_element_type=f32)
    t_io = lax.broadcasted_iota(i32, (sub, 1), 0)
    outs = []
    for i in range(c // sub):
        lo, hi = i * sub, (i + 1) * sub
        q_i, k_i, cum_i, v_i = q[lo:hi], k[lo:hi], cum[lo:hi], v[lo:hi].astype(f32)
        o_i = jnp.zeros((sub, GLA_DV), f32)
        klo, khi = (hi, c) if rev else (0, lo)
        if khi > klo:
            edge = cum[hi:hi + 1] if rev else cum[lo - 1:lo]
            qe = (q_i * jnp.exp(cum_i - edge)).astype(bf16)
            ke = (k[klo:khi] * jnp.exp(edge - cum[klo:khi])).astype(bf16)
            att = lax.dot_general(qe, ke, NT_DIMS, preferred_element_type=f32)
            o_i = o_i + jnp.dot(att.astype(bf16), v[klo:khi], preferred_element_type=f32)
        for s in range(sub):
            ok = (t_io <= s) if rev else (t_io >= s)
            e = jnp.exp(jnp.where(ok, cum_i - cum_i[s:s + 1], NEG))
            a_col = jnp.sum(q_i * k_i[s:s + 1] * e, axis=-1, keepdims=True)
            o_i = o_i + a_col * v_i[s:s + 1]
        outs.append(o_i)
    return o_inter + jnp.concatenate(outs, axis=0), st_new


def _gla_kernel(qf_ref, kf_ref, vf_ref, ddf_ref, qb_ref, kb_ref, vb_ref, ddb_ref, kc_ref, vc_ref, ddc_ref,
                upf_ref, upb_ref, bf_ref, bb_ref, of_ref, ob_ref, sf_scr, sb_scr):
    j = pl.program_id(2)
    n_chunks = kc_ref.shape[0] // GLA_CHUNK
    upf, upb, bias_f, bias_b = upf_ref[0], upb_ref[0], bf_ref[...], bb_ref[...]

    def rows(ref, ci):
        return ref[ci * GLA_CHUNK:(ci + 1) * GLA_CHUNK, :]

    @pl.when(j == 0)
    def _():
        sf = jnp.zeros((GLA_DV, GLA_DK), f32)
        sb = jnp.zeros((GLA_DV, GLA_DK), f32)
        for ci in range(n_chunks):
            cb = n_chunks - 1 - ci
            gf = _log_decay(rows(ddc_ref, ci), upf, bias_f)
            gb = _log_decay(rows(ddc_ref, cb), upb, bias_b)
            _, sf = _gla_chunk(None, rows(kc_ref, ci).astype(f32), rows(vc_ref, ci), gf, sf, rev=False,
                               need_out=False)
            _, sb = _gla_chunk(None, rows(kc_ref, cb).astype(f32), rows(vc_ref, cb), gb, sb, rev=True,
                               need_out=False)
        sf_scr[...] = sf
        sb_scr[...] = sb

    @pl.when(j > 0)
    def _():
        sf = sf_scr[...]
        sb = sb_scr[...]
        nc = kf_ref.shape[0] // GLA_CHUNK
        for ci in range(nc):
            cb = nc - 1 - ci
            gf = _log_decay(rows(ddf_ref, ci), upf, bias_f)
            gb = _log_decay(rows(ddb_ref, cb), upb, bias_b)
            o_f, sf = _gla_chunk(rows(qf_ref, ci).astype(f32), rows(kf_ref, ci).astype(f32), rows(vf_ref, ci), gf,
                                 sf, rev=False, need_out=True)
            o_b, sb = _gla_chunk(rows(qb_ref, cb).astype(f32), rows(kb_ref, cb).astype(f32), rows(vb_ref, cb), gb,
                                 sb, rev=True, need_out=True)
            of_ref[ci * GLA_CHUNK:(ci + 1) * GLA_CHUNK, :] = o_f.astype(bf16)
            ob_ref[cb * GLA_CHUNK:(cb + 1) * GLA_CHUNK, :] = o_b.astype(bf16)
        sf_scr[...] = sf
        sb_scr[...] = sb


def _gla(p_lat, dd_lat, p_ctx, dd_ctx, upf, upb, bias_f, bias_b, batch, l, lc):
    tt = lc
    n_l = l // tt
    hq, hk, hv = COL_GQ // GLA_DK, COL_GK // GLA_DK, COL_GV // GLA_DV

    def fwd(b, h, j):
        return b * n_l + jnp.maximum(j - 1, 0)

    def bwd(b, h, j):
        return b * n_l + n_l - jnp.maximum(j, 1)

    lat = lambda rowmap, width, col: pl.BlockSpec((tt, width), lambda b, h, j: (rowmap(b, h, j), col + h))
    ddspec = lambda rowmap: pl.BlockSpec((tt, 128), lambda b, h, j: (rowmap(b, h, j), 0))
    ctx = lambda width, col: pl.BlockSpec((tt, width), lambda b, h, j: (b, col + h))
    return pl.pallas_call(
        _gla_kernel,
        out_shape=[jax.ShapeDtypeStruct((batch * l, GLA_VAL_WIDTH), bf16)] * 2,
        grid=(batch, GLA_HEADS, n_l + 1),
        in_specs=[lat(fwd, GLA_DK, hq), lat(fwd, GLA_DK, hk), lat(fwd, GLA_DV, hv), ddspec(fwd),
                  lat(bwd, GLA_DK, hq), lat(bwd, GLA_DK, hk), lat(bwd, GLA_DV, hv), ddspec(bwd),
                  ctx(GLA_DK, hk), ctx(GLA_DV, hv), pl.BlockSpec((tt, 128), lambda b, h, j: (b, 0)),
                  pl.BlockSpec((1, 128, GLA_DK), lambda b, h, j: (h, 0, 0)),
                  pl.BlockSpec((1, 128, GLA_DK), lambda b, h, j: (h, 0, 0)),
                  pl.BlockSpec((1, GLA_DK), lambda b, h, j: (0, h)),
                  pl.BlockSpec((1, GLA_DK), lambda b, h, j: (0, h))],
        out_specs=[pl.BlockSpec((tt, GLA_DV), lambda b, h, j: (fwd(b, h, j), h)),
                   pl.BlockSpec((tt, GLA_DV), lambda b, h, j: (bwd(b, h, j), h))],
        scratch_shapes=[pltpu.VMEM((GLA_DV, GLA_DK), f32), pltpu.VMEM((GLA_DV, GLA_DK), f32)],
        compiler_params=_cparams(("arbitrary", "arbitrary", "arbitrary")),
        name="gla",
    )(p_lat, p_lat, p_lat, dd_lat, p_lat, p_lat, p_lat, dd_lat, p_ctx, p_ctx, dd_ctx, upf, upb, bias_f, bias_b)


def _pad_gate_up(up, row0):
    w = up.reshape(GLA_GATE_RANK, GLA_HEADS, GLA_DK).transpose(1, 0, 2)
    return jnp.zeros((GLA_HEADS, 128, GLA_DK), f32).at[:, row0:row0 + GLA_GATE_RANK].set(w).astype(bf16)


def _outproj_kernel(ona_ref, of_ref, ob_ref, gate_ref, gng_ref, w_ref, x_ref, g1_ref, n2g_ref, sc2_ref, sh2_ref,
                    rwt_ref, g2_ref, ws1_ref, ws3_ref, ws2_ref, x1s_ref, h2_ref, lg_ref):
    o = of_ref[...].astype(f32) + ob_ref[...].astype(f32)
    gla = (_head_rms(o, gng_ref[...], GLA_DV) * gate_ref[...].astype(f32)).astype(bf16)
    acc = (jnp.dot(ona_ref[...], w_ref[0:NA_WIDTH, :], preferred_element_type=f32)
           + jnp.dot(gla, w_ref[NA_WIDTH:, :], preferred_element_type=f32))
    x1 = x_ref[...] + g1_ref[0] * acc
    ms = jnp.mean(x1 * x1, axis=-1, keepdims=True)
    h2 = x1 * lax.rsqrt(ms + NORM_EPS) * n2g_ref[...] * (1.0 + sc2_ref[0]) + sh2_ref[0]
    lg_ref[...] = lax.dot_general(rwt_ref[...], h2, NT_DIMS, precision=lax.Precision.HIGHEST,
                                  preferred_element_type=f32)
    h2_ref[:, 0, :] = h2
    hb = h2.astype(bf16)
    s1 = jnp.dot(hb, ws1_ref[...], preferred_element_type=f32)
    s3 = jnp.dot(hb, ws3_ref[...], preferred_element_type=f32)
    act = (s1 * jax.nn.sigmoid(s1) * s3).astype(bf16)
    x1s_ref[...] = x1 + g2_ref[0] * jnp.dot(act, ws2_ref[...], preferred_element_type=f32)


def _outproj(o_na, o_f, o_b, p_lat, gnorm_g, w_out_bf, x2d, g1, norm2_g, sc2, sh2, rwt, g2, ws1, ws3, ws2,
             tiles_per_batch, tm):
    m, d = x2d.shape
    vec = lambda i: (0, 0)
    per_b = lambda i: (i // tiles_per_batch, 0, 0)
    return pl.pallas_call(
        _outproj_kernel,
        out_shape=[jax.ShapeDtypeStruct((m, d), f32), jax.ShapeDtypeStruct((m, 1, d), f32),
                   jax.ShapeDtypeStruct((N_EXPERTS, m), f32)],
        grid=(m // tm,),
        in_specs=[pl.BlockSpec((tm, NA_WIDTH), lambda i: (i, 0)),
                  pl.BlockSpec((tm, GLA_VAL_WIDTH), lambda i: (i, 0)),
                  pl.BlockSpec((tm, GLA_VAL_WIDTH), lambda i: (i, 0)),
                  pl.BlockSpec((tm, GLA_VAL_WIDTH), lambda i: (i, COL_GATE // GLA_VAL_WIDTH)),
                  pl.BlockSpec((1, GLA_DV), vec),
                  pl.BlockSpec(w_out_bf.shape, vec),
                  pl.BlockSpec((tm, d), lambda i: (i, 0)),
                  pl.BlockSpec((1, 1, d), per_b),
                  pl.BlockSpec((1, d), vec),
                  pl.BlockSpec((1, 1, d), per_b),
                  pl.BlockSpec((1, 1, d), per_b),
                  pl.BlockSpec((N_EXPERTS, d), vec),
                  pl.BlockSpec((1, 1, d), per_b),
                  pl.BlockSpec(ws1.shape, vec),
                  pl.BlockSpec(ws3.shape, vec),
                  pl.BlockSpec(ws2.shape, vec)],
        out_specs=[pl.BlockSpec((tm, d), lambda i: (i, 0)),
                   pl.BlockSpec((tm, 1, d), lambda i: (i, 0, 0)),
                   pl.BlockSpec((N_EXPERTS, tm), lambda i: (0, i))],
        compiler_params=_cparams(("arbitrary",)),
        name="outproj",
    )(o_na, o_f, o_b, p_lat, gnorm_g, w_out_bf, x2d, g1, norm2_g, sc2, sh2, rwt, g2, ws1, ws3, ws2)


def _first_argmax(vals, iota, axis, sentinel):
    mx = jnp.max(vals, axis=axis, keepdims=True)
    am = jnp.min(jnp.where(vals == mx, iota, sentinel), axis=axis, keepdims=True)
    return mx, am


def _route_tile(lg, rb):
    t = lg.shape[1]
    per = N_EXPERTS // N_GROUPS
    scores = jax.nn.sigmoid(lg)
    biased = scores + rb
    b3 = biased.reshape(N_GROUPS, per, t)
    io3 = lax.broadcasted_iota(i32, (N_GROUPS, per, t), 1)
    m1, a1 = _first_argmax(b3, io3, 1, per)
    m2 = jnp.max(jnp.where(io3 == a1, -jnp.inf, b3), axis=1, keepdims=True)
    gs = (m1 + m2).reshape(N_GROUPS, t)
    iog = lax.broadcasted_iota(i32, (N_GROUPS, t), 0)
    gsel = jnp.zeros((N_GROUPS, t), jnp.bool_)
    for _ in range(TOPK_GROUPS):
        _, am = _first_argmax(gs, iog, 0, N_GROUPS)
        hit = iog == am
        gsel = gsel | hit
        gs = jnp.where(hit, -jnp.inf, gs)
    emask = jnp.broadcast_to(gsel[:, None, :], (N_GROUPS, per, t)).reshape(N_EXPERTS, t)
    masked = jnp.where(emask, biased, -jnp.inf)
    ioe = lax.broadcasted_iota(i32, (N_EXPERTS, t), 0)
    hits, sels = [], []
    for _ in range(TOP_K):
        _, am = _first_argmax(masked, ioe, 0, N_EXPERTS)
        hit = ioe == am
        hits.append(hit)
        sels.append(jnp.sum(jnp.where(hit, scores, 0.0), axis=0, keepdims=True))
        masked = jnp.where(hit, -jnp.inf, masked)
    denom = sels[0]
    for s in sels[1:]:
        denom = denom + s
    wts = [s / denom * ROUTED_SCALE for s in sels]
    return hits, wts


def _route_kernel(lg_ref, rb_ref, pos_ref, wts_ref, meta_ref, blk_ref, cnt_scr, carry_scr, base_scr, tri_scr):
    ph = pl.program_id(0)
    i = pl.program_id(1)
    t = lg_ref.shape[1]
    hits, wts = _route_tile(lg_ref[...], rb_ref[...])
    sel = hits[0]
    for h in hits[1:]:
        sel = sel | h
    self32 = sel.astype(f32)
    tile_cnt = jnp.sum(self32, axis=1, keepdims=True)

    @pl.when((ph == 0) & (i == 0))
    def _():
        cnt_scr[...] = jnp.zeros_like(cnt_scr)

    @pl.when(ph == 0)
    def _():
        cnt_scr[...] += jnp.broadcast_to(tile_cnt, cnt_scr.shape)

    @pl.when((ph == 1) & (i == 0))
    def _():
        cnt = cnt_scr[...]
        nblk = jnp.floor((cnt + (MOE_TM - 1)) * (1.0 / MOE_TM))
        r = lax.broadcasted_iota(i32, (N_EXPERTS, N_EXPERTS), 0)
        c = lax.broadcasted_iota(i32, (N_EXPERTS, N_EXPERTS), 1)
        excl = jnp.dot((c < r).astype(bf16), nblk.astype(bf16), preferred_element_type=f32)
        base_scr[...] = excl * MOE_TM
        carry_scr[...] = jnp.zeros_like(carry_scr)
        meta_ref[0] = cnt.astype(i32)
        meta_ref[1] = (excl * MOE_TM).astype(i32)
        incl = excl + nblk
        nb = blk_ref.shape[1]
        jj = lax.broadcasted_iota(i32, (N_EXPERTS, nb), 1).astype(f32)
        owner = jnp.sum((jnp.broadcast_to(incl[:, 0:1], (N_EXPERTS, nb)) <= jj).astype(i32), axis=0, keepdims=True)
        blk_ref[0:1, :] = jnp.minimum(owner, N_EXPERTS - 1)
        blk_ref[1:2, :] = jnp.broadcast_to(incl[N_EXPERTS - 1:N_EXPERTS, 0:1].astype(i32), (1, nb))
        blk_ref[2:8, :] = jnp.zeros((6, nb), i32)
        rr = lax.broadcasted_iota(i32, (t, t), 0)
        cc = lax.broadcasted_iota(i32, (t, t), 1)
        tri_scr[...] = (rr < cc).astype(bf16)

    @pl.when(ph == 1)
    def _():
        rank = jnp.dot(sel.astype(bf16), tri_scr[...], preferred_element_type=f32)
        posf = base_scr[:, 0:1] + carry_scr[:, 0:1] + rank
        for k in range(TOP_K):
            pos_ref[k:k + 1, :] = jnp.sum(jnp.where(hits[k], posf, 0.0), axis=0, keepdims=True).astype(i32)
            wts_ref[k:k + 1, :] = wts[k]
        pos_ref[TOP_K:8, :] = jnp.zeros((8 - TOP_K, t), i32)
        wts_ref[TOP_K:8, :] = jnp.zeros((8 - TOP_K, t), f32)
        carry_scr[...] += jnp.broadcast_to(tile_cnt, carry_scr.shape)


def _route(logits_t, router_bias, n_blocks):
    n_tok = logits_t.shape[1]
    tt = 1024
    n_tiles = n_tok // tt
    return pl.pallas_call(
        _route_kernel,
        out_shape=[jax.ShapeDtypeStruct((8, n_tok), i32), jax.ShapeDtypeStruct((8, n_tok), f32),
                   jax.ShapeDtypeStruct((2, N_EXPERTS, 128), i32), jax.ShapeDtypeStruct((8, n_blocks), i32)],
        grid=(2, n_tiles),
        in_specs=[pl.BlockSpec((N_EXPERTS, tt), lambda p, i: (0, i)),
                  pl.BlockSpec((N_EXPERTS, 1), lambda p, i: (0, 0))],
        out_specs=[pl.BlockSpec((8, tt), lambda p, i: (0, i * p)),
                   pl.BlockSpec((8, tt), lambda p, i: (0, i * p)),
                   pl.BlockSpec((2, N_EXPERTS, 128), lambda p, i: (0, 0, 0)),
                   pl.BlockSpec((8, n_blocks), lambda p, i: (0, 0))],
        scratch_shapes=[pltpu.VMEM((N_EXPERTS, 128), f32), pltpu.VMEM((N_EXPERTS, 128), f32),
                        pltpu.VMEM((N_EXPERTS, 128), f32), pltpu.VMEM((tt, tt), bf16)],
        compiler_params=_cparams(("arbitrary", "arbitrary")),
        name="route",
    )(logits_t, router_bias.reshape(N_EXPERTS, 1))


def _experts_kernel(blk_e_ref, nblk_ref, pos_ref, cnt_ref, base_ref, h2_ref, w1_hbm, w3_hbm, w2_hbm, ys_ref,
                    tok_smem, ord_smem, xbuf, sems, wbuf1, wbuf3, wbuf2, wsems, w1_scr, w3_scr, w2_scr):
    j = pl.program_id(0)
    n_valid = nblk_ref[0]
    n_tok = pos_ref.shape[0] // TOP_K
    n_slots = GATHER_AHEAD + 1

    def row_copy(row, slot, r):
        return pltpu.make_async_copy(h2_ref.at[tok_smem[row]], xbuf.at[slot, pl.ds(r, 1)], sems.at[slot])

    def weight_copies(expert, slot):
        return [pltpu.make_async_copy(hbm.at[expert], buf.at[slot], wsems.at[slot])
                for hbm, buf in ((w1_hbm, wbuf1), (w3_hbm, wbuf3), (w2_hbm, wbuf2))]

    def start_gather(block, slot):
        def issue(r, carry):
            row_copy(block * MOE_TM + r, slot, r).start()
            return carry

        lax.fori_loop(0, MOE_TM, issue, 0, unroll=8)

    def drain_gather(slot):
        def drain(r, carry):
            row_copy(0, slot, 0).wait()
            return carry

        lax.fori_loop(0, MOE_TM, drain, 0, unroll=8)

    @pl.when(j == 0)
    def _():
        def pad_expert(e, carry):
            cnt = cnt_ref[e]
            first = base_ref[e] + cnt

            def fill(r, c):
                tok_smem[first + r] = 0
                return c

            lax.fori_loop(0, (MOE_TM - cnt % MOE_TM) % MOE_TM, fill, 0)
            return carry

        lax.fori_loop(0, N_EXPERTS, pad_expert, 0)

        def invert(t, carry):
            for k in range(TOP_K):
                tok_smem[pos_ref[k * n_tok + t]] = t
            return carry

        lax.fori_loop(0, n_tok, invert, 0, unroll=4)
        ord_smem[0] = 0
        for c in weight_copies(blk_e_ref[0], 0):
            c.start()
        for ahead in range(GATHER_AHEAD):
            start_gather(jnp.minimum(ahead, n_valid - 1), ahead)

    e = blk_e_ref[jnp.minimum(j, n_valid - 1)]
    e_prev = blk_e_ref[jnp.maximum(j - 1, 0)]

    @pl.when((j == 0) | ((j < n_valid) & (e != e_prev)))
    def _():
        ordinal = ord_smem[0]
        slot = ordinal % 2
        for c in weight_copies(e, slot):
            c.wait()
        next_blk = (base_ref[e] + cnt_ref[e] + (MOE_TM - 1)) // MOE_TM

        @pl.when(next_blk < n_valid)
        def _():
            for c in weight_copies(blk_e_ref[jnp.minimum(next_blk, n_valid - 1)], 1 - slot):
                c.start()

        w1_scr[...] = wbuf1[slot].astype(bf16)
        w3_scr[...] = wbuf3[slot].astype(bf16)
        w2_scr[...] = wbuf2[slot].astype(bf16)
        ord_smem[0] = ordinal + 1

    @pl.when(j < n_valid)
    def _():
        slot = j % n_slots
        drain_gather(slot)
        nxt = jnp.minimum(j + GATHER_AHEAD, n_valid - 1) * MOE_TM
        nxt_slot = (j + GATHER_AHEAD) % n_slots
        for r in range(MOE_TM):
            row_copy(nxt + r, nxt_slot, r).start(priority=r % 2)
        x = xbuf[slot].astype(bf16)
        h1 = jnp.dot(x, w1_scr[...], preferred_element_type=f32)
        h3 = jnp.dot(x, w3_scr[...], preferred_element_type=f32)
        act = (h1 * jax.nn.sigmoid(h1) * h3).astype(bf16)
        ys_ref[:, 0, :] = jnp.dot(act, w2_scr[...], preferred_element_type=f32)

    @pl.when(j == n_valid - 1)
    def _():
        for ahead in range(1, GATHER_AHEAD + 1):
            drain_gather((j + ahead) % n_slots)

    @pl.when(j >= n_valid)
    def _():
        ys_ref[...] = jnp.zeros_like(ys_ref)


def _experts(blk_expert, nblk, pos_flat, cnt, base, h2, w1, w3, w2, n_blocks):
    d, de = w1.shape[1], w1.shape[2]
    n_rows = n_blocks * MOE_TM

    hbm = pl.BlockSpec(memory_space=pl.ANY)
    return pl.pallas_call(
        _experts_kernel,
        out_shape=jax.ShapeDtypeStruct((n_rows, 1, d), f32),
        grid_spec=pltpu.PrefetchScalarGridSpec(
            num_scalar_prefetch=5,
            grid=(n_blocks,),
            in_specs=[hbm, hbm, hbm, hbm],
            out_specs=pl.BlockSpec((MOE_TM, 1, d), lambda j, *_: (j, 0, 0)),
            scratch_shapes=[pltpu.SMEM((n_rows,), i32), pltpu.SMEM((1,), i32),
                            pltpu.VMEM((GATHER_AHEAD + 1, MOE_TM, d), f32),
                            pltpu.SemaphoreType.DMA((GATHER_AHEAD + 1,)),
                            pltpu.VMEM((2, d, de), f32), pltpu.VMEM((2, d, de), f32), pltpu.VMEM((2, de, d), f32),
                            pltpu.SemaphoreType.DMA((2,)),
                            pltpu.VMEM((d, de), bf16), pltpu.VMEM((d, de), bf16), pltpu.VMEM((de, d), bf16)]),
        compiler_params=_cparams(("arbitrary",)),
        name="experts",
    )(blk_expert, nblk, pos_flat, cnt, base, h2, w1, w3, w2)


def _combine_kernel(pos_ref, ys_ref, wts_ref, x1s_ref, g2_ref, o_ref, ybuf, sems):
    i = pl.program_id(0)
    n_tiles = 2 * pl.num_programs(0)
    n_tok = pos_ref.shape[0] // TOP_K
    tm = COMBINE_TM

    def row_copy(src_row, slot, k, t):
        return pltpu.make_async_copy(ys_ref.at[src_row], ybuf.at[slot, k, pl.ds(t, 1)], sems.at[slot])

    def drain_gather(slot):
        def drain(t, carry):
            for k in range(TOP_K):
                row_copy(0, slot, k, 0).wait()
            return carry

        lax.fori_loop(0, tm, drain, 0, unroll=2)

    def half_step(slot, next_tile):
        drain_gather(slot)
        for t in range(tm):
            for k in range(TOP_K):
                row_copy(pos_ref[k * n_tok + next_tile * tm + t], 1 - slot, k, t).start(priority=(t + k) % 2)
        rows = slice(slot * tm, (slot + 1) * tm)
        w = wts_ref[rows, :]
        acc = w[:, 0:1] * ybuf[slot, 0]
        for k in range(1, TOP_K):
            acc = acc + w[:, k:k + 1] * ybuf[slot, k]
        o_ref[rows, :] = x1s_ref[rows, :] + g2_ref[0] * acc

    @pl.when(i == 0)
    def _():
        def issue(t, carry):
            for k in range(TOP_K):
                row_copy(pos_ref[k * n_tok + t], 0, k, t).start()
            return carry

        lax.fori_loop(0, tm, issue, 0, unroll=2)

    half_step(0, 2 * i + 1)
    half_step(1, jnp.minimum(2 * i + 2, n_tiles - 1))

    @pl.when(i == pl.num_programs(0) - 1)
    def _():
        drain_gather(0)


def _combine(pos_flat, ys, wts_t, x1s, g2, tokens_per_batch):
    n_tok, d = x1s.shape
    tm = 2 * COMBINE_TM
    return pl.pallas_call(
        _combine_kernel,
        out_shape=jax.ShapeDtypeStruct((n_tok, d), f32),
        grid=(n_tok // tm,),
        in_specs=[pl.BlockSpec(memory_space=pltpu.SMEM),
                  pl.BlockSpec(memory_space=pl.ANY),
                  pl.BlockSpec((tm, 8), lambda i: (i, 0)),
                  pl.BlockSpec((tm, d), lambda i: (i, 0)),
                  pl.BlockSpec((1, 1, d), lambda i: (i // (tokens_per_batch // tm), 0, 0))],
        out_specs=pl.BlockSpec((tm, d), lambda i: (i, 0)),
        scratch_shapes=[pltpu.VMEM((2, TOP_K, COMBINE_TM, d), f32), pltpu.SemaphoreType.DMA((2,))],
        compiler_params=_cparams(("arbitrary",)),
        name="combine",
    )(pos_flat, ys, wts_t, x1s, g2)


def _layer(x, ctx, mod, norm1_g, norm2_g, w_in, q_norm_g, k_norm_g, na_rpb, up_f, bias_f, up_b, bias_b, gla_norm_g,
           w_out, router_w, router_bias, w1, w3, w2, ws1, ws3, ws2):
    batch, l, d = x.shape
    lc = ctx.shape[1]
    n_tok = batch * l
    sh1, sc1, g1, sh2, sc2, g2 = [mod[:batch, k * d:(k + 1) * d].reshape(batch, 1, d) for k in range(6)]
    csh1, csc1 = [mod[batch:batch + 1, k * d:(k + 1) * d].reshape(1, 1, d) for k in range(2)]

    w_bf = w_in[:, :COL_DD].astype(bf16)
    wdd_bf = jnp.zeros((d, 128), f32).at[:, :2 * GLA_GATE_RANK].set(w_in[:, COL_DD:]).astype(bf16)
    qg, kg = q_norm_g.reshape(1, -1), k_norm_g.reshape(1, -1)
    tabs = _rope_tables(l)
    n1 = norm1_g.reshape(1, d)
    tm_lat = min(512, l)
    p_lat, dd_lat = _inproj(x.reshape(n_tok, d), sc1, sh1, n1, w_bf, wdd_bf, qg, kg, tabs, tm=tm_lat,
                            tiles_per_group=l // tm_lat, rope=True)
    tm_ctx = batch * lc
    ctabs = tuple(t[:tm_ctx] for t in tabs)
    p_ctx, dd_ctx = _inproj(ctx.reshape(batch * lc, d), csc1, csh1, n1, w_bf, wdd_bf, qg, kg, ctabs, tm=tm_ctx,
                            tiles_per_group=1, rope=False)

    bias = _na_bias_tables(na_rpb, l // GRID_W)
    o_na = _na(p_lat, p_ctx, bias, batch, l, lc)

    o_f, o_b = _gla(p_lat, dd_lat, p_ctx, dd_ctx, _pad_gate_up(up_f, 0), _pad_gate_up(up_b, GLA_GATE_RANK),
                    bias_f.reshape(1, -1), bias_b.reshape(1, -1), batch, l, lc)

    tm_out = 256
    x1s, h2, logits_t = _outproj(o_na, o_f, o_b, p_lat, gla_norm_g.reshape(1, -1), w_out.astype(bf16),
                                 x.reshape(n_tok, d), g1, norm2_g.reshape(1, d), sc2, sh2, router_w.T, g2,
                                 ws1.astype(bf16), ws3.astype(bf16), ws2.astype(bf16), l // tm_out, tm_out)

    n_blocks = n_tok * TOP_K // MOE_TM + N_EXPERTS
    pos, wts, meta, blk = _route(logits_t, router_bias, n_blocks)
    pos_flat = pos[:TOP_K].reshape(-1)
    ys = _experts(blk[0], blk[1, :1], pos_flat, meta[0, :, 0], meta[1, :, 0], h2, w1, w3, w2, n_blocks)
    out = _combine(pos_flat, ys, wts.T, x1s, g2, l)
    return out.reshape(batch, l, d)


def kernel(x, c, ctx, c_ctx, w_mod, b_mod, norm1_g, norm2_g, w_in, q_norm_g, k_norm_g, na_rpb, gla_gate_up_f,
           gla_gate_bias_f, gla_gate_up_b, gla_gate_bias_b, gla_norm_g, w_out, router_w, router_bias, expert_w1,
           expert_w3, expert_w2, shared_w1, shared_w3, shared_w2):
    depth = w_mod.shape[0]
    assert depth == 1, "context-stream outputs are only dropped for a single (last) layer"
    batch, d = c.shape
    cond8 = jnp.zeros((8, d), f32).at[:batch].set(c).at[batch].set(c_ctx)
    mod = _adaln(cond8, w_mod[0], b_mod[0])
    return _layer(x, ctx, mod, norm1_g[0], norm2_g[0], w_in[0], q_norm_g[0], k_norm_g[0], na_rpb[0],
                  gla_gate_up_f[0], gla_gate_bias_f[0], gla_gate_up_b[0], gla_gate_bias_b[0], gla_norm_g[0],
                  w_out[0], router_w[0], router_bias[0], expert_w1[0], expert_w3[0], expert_w2[0], shared_w1[0],
                  shared_w3[0], shared_w2[0])
```

```python
import functools

import jax
import jax.numpy as jnp
import numpy as np
from jax import lax
from jax.experimental import pallas as pl
from jax.experimental.pallas import tpu as pltpu

f32 = jnp.float32
bf16 = jnp.bfloat16
i32 = jnp.int32

GRID_W = 64
NORM_EPS = 1e-6
NA_HEADS = 8
NA_HEAD_DIM = 128
NA_WIN_H = 8
NA_WIN_W = 16
GLA_HEADS = 4
GLA_DK = 128
GLA_DV = 256
GLA_GATE_RANK = 16
GLA_GATE_TAU = 16.0
GLA_CHUNK = 64
GLA_SUB = 16
ROPE_BASE = 10000.0
N_EXPERTS = 64
N_GROUPS = 8
TOPK_GROUPS = 4
TOP_K = 6
D_EXPERT = 512
ROUTED_SCALE = 2.5

NA_WIDTH = NA_HEADS * NA_HEAD_DIM
GLA_KEY_WIDTH = GLA_HEADS * GLA_DK
GLA_VAL_WIDTH = GLA_HEADS * GLA_DV
COL_NQ, COL_NK, COL_NV = 0, NA_WIDTH, 2 * NA_WIDTH
COL_GQ = 3 * NA_WIDTH
COL_GK = COL_GQ + GLA_KEY_WIDTH
COL_GV = COL_GK + GLA_KEY_WIDTH
COL_GATE = COL_GV + GLA_VAL_WIDTH
COL_DD = COL_GATE + GLA_VAL_WIDTH
PROJ_TN = 512
INPROJ_TM = 256
NA_QROWS = 4
NA_KROWS = 12
NA_HEADS_PER_STEP = 4
MOE_TM = 256
COMBINE_TM = 128
GATHER_AHEAD = 2
NEG = -1e30

VMEM_LIMIT = 56 * 1024 * 1024


def _cparams(sem, vmem=VMEM_LIMIT):
    return pltpu.CompilerParams(dimension_semantics=sem, vmem_limit_bytes=vmem)


NT_DIMS = (((1,), (1,)), ((), ()))
TN_DIMS = (((0,), (0,)), ((), ()))


def _mod_kernel(c_ref, w_ref, b_ref, o_ref):
    c = c_ref[...]
    s = (c * jax.nn.sigmoid(c)).astype(bf16)
    o_ref[...] = jnp.dot(s, w_ref[...].astype(bf16), preferred_element_type=f32) + b_ref[...]


def _adaln(cond8, w_mod, b_mod):
    d, n = w_mod.shape
    tn = 1024
    return pl.pallas_call(
        _mod_kernel,
        out_shape=jax.ShapeDtypeStruct((8, n), f32),
        grid=(n // tn,),
        in_specs=[pl.BlockSpec((8, d), lambda j: (0, 0)),
                  pl.BlockSpec((d, tn), lambda j: (0, j)),
                  pl.BlockSpec((1, tn), lambda j: (0, j))],
        out_specs=pl.BlockSpec((8, tn), lambda j: (0, j)),
        compiler_params=_cparams(("arbitrary",)),
        name="mod",
    )(cond8, w_mod, b_mod.reshape(1, n))


def _head_rms(a, g, width):
    parts = []
    for h in range(a.shape[-1] // width):
        ah = a[:, h * width:(h + 1) * width]
        ms = jnp.mean(ah * ah, axis=-1, keepdims=True)
        parts.append(ah * lax.rsqrt(ms + NORM_EPS) * g)
    return jnp.concatenate(parts, axis=-1)


def _rope(a, cos, sin_lo, sin_hi):
    parts = []
    for h in range(a.shape[-1] // GLA_DK):
        ah = a[:, h * GLA_DK:(h + 1) * GLA_DK]
        parts.append(ah * cos + pltpu.roll(ah, 32, axis=1) * sin_hi + pltpu.roll(ah, 96, axis=1) * sin_lo)
    return jnp.concatenate(parts, axis=-1)


def _inproj_kernel(x_ref, sc_ref, sh_ref, g_ref, w_hbm, wdd_ref, qg_ref, kg_ref, cos_ref, slo_ref, shi_ref,
                   o_ref, dd_ref, w_scr, sem, *, rope):
    @pl.when(pl.program_id(0) == 0)
    def _():
        fetch = pltpu.make_async_copy(w_hbm, w_scr, sem)
        fetch.start()
        fetch.wait()

    x = x_ref[...]
    ms = jnp.mean(x * x, axis=-1, keepdims=True)
    y = x * lax.rsqrt(ms + NORM_EPS) * g_ref[...]
    hb = (y * (1.0 + sc_ref[0]) + sh_ref[0]).astype(bf16)
    dd_ref[...] = jnp.dot(hb, wdd_ref[...], preferred_element_type=f32)

    def epilogue(col, acc):
        if col < COL_NK:
            return _head_rms(acc, qg_ref[...], NA_HEAD_DIM)
        if col < COL_NV:
            return _head_rms(acc, kg_ref[...], NA_HEAD_DIM)
        if col == COL_GQ:
            a = _rope(acc, cos_ref[...], slo_ref[...], shi_ref[...]) if rope else acc
            return a * (GLA_DK ** -0.5)
        if col == COL_GK:
            return _rope(acc, cos_ref[...], slo_ref[...], shi_ref[...]) if rope else acc
        if col >= COL_GATE:
            return acc * jax.nn.sigmoid(acc)
        return acc

    for col in range(0, COL_DD, PROJ_TN):
        acc = jnp.dot(hb, w_scr[:, col:col + PROJ_TN], preferred_element_type=f32)
        o_ref[:, col:col + PROJ_TN] = epilogue(col, acc).astype(bf16)


def _inproj(x2d, sc, sh, norm_g, w_bf, wdd_bf, qg, kg, rope_tabs, *, tm, tiles_per_group, rope):
    m, d = x2d.shape
    cos, slo, shi = rope_tabs
    rope_tiles = cos.shape[0] // tm
    vec = lambda i: (0, 0)
    return pl.pallas_call(
        functools.partial(_inproj_kernel, rope=rope),
        out_shape=[jax.ShapeDtypeStruct((m, COL_DD), bf16), jax.ShapeDtypeStruct((m, 128), f32)],
        grid=(m // tm,),
        in_specs=[pl.BlockSpec((tm, d), lambda i: (i, 0)),
                  pl.BlockSpec((1, 1, d), lambda i: (i // tiles_per_group, 0, 0)),
                  pl.BlockSpec((1, 1, d), lambda i: (i // tiles_per_group, 0, 0)),
                  pl.BlockSpec((1, d), vec),
                  pl.BlockSpec(memory_space=pl.ANY),
                  pl.BlockSpec((d, 128), vec),
                  pl.BlockSpec((1, NA_HEAD_DIM), vec),
                  pl.BlockSpec((1, NA_HEAD_DIM), vec),
                  pl.BlockSpec((tm, GLA_DK), lambda i: (i % rope_tiles, 0)),
                  pl.BlockSpec((tm, GLA_DK), lambda i: (i % rope_tiles, 0)),
                  pl.BlockSpec((tm, GLA_DK), lambda i: (i % rope_tiles, 0))],
        out_specs=[pl.BlockSpec((tm, COL_DD), lambda i: (i, 0)),
                   pl.BlockSpec((tm, 128), lambda i: (i, 0))],
        scratch_shapes=[pltpu.VMEM(w_bf.shape, bf16), pltpu.SemaphoreType.DMA(())],
        compiler_params=_cparams(("arbitrary",)),
        name="inproj_rope" if rope else "inproj_ctx",
    )(x2d, sc, sh, norm_g, w_bf, wdd_bf, qg, kg, cos, slo, shi)


def _rope_tables(l):
    t = np.arange(l)
    half = GLA_DK // 4
    inv_freq = ROPE_BASE ** (-np.arange(half, dtype=np.float64) / half)
    ang_r = (t // GRID_W)[:, None] * inv_freq[None, :]
    ang_c = (t % GRID_W)[:, None] * inv_freq[None, :]
    cos = np.concatenate([np.cos(ang_r)] * 2 + [np.cos(ang_c)] * 2, axis=-1)
    sin = np.concatenate([np.sin(ang_r)] * 2 + [np.sin(ang_c)] * 2, axis=-1)
    lane = np.arange(GLA_DK) % (2 * half)
    sin_lo = np.where(lane[None, :] < half, -sin, 0.0)
    sin_hi = np.where(lane[None, :] >= half, sin, 0.0)
    return tuple(jnp.asarray(a, f32) for a in (cos, sin_lo, sin_hi))


def _na_kernel(q_ref, k_ref, v_ref, kc_ref, vc_ref, bias_ref, o_ref, *, rows):
    a = pl.program_id(2)
    start = pl.multiple_of(jnp.clip(NA_QROWS * a - NA_WIN_H // 2, 0, rows - NA_KROWS) * GRID_W, GRID_W)
    scale = NA_HEAD_DIM ** -0.5
    for h in range(NA_HEADS_PER_STEP):
        lanes = slice(h * NA_HEAD_DIM, (h + 1) * NA_HEAD_DIM)
        q = q_ref[:, lanes]
        k = k_ref[pl.ds(start, NA_KROWS * GRID_W), lanes]
        v = v_ref[pl.ds(start, NA_KROWS * GRID_W), lanes]
        s_loc = lax.dot_general(q, k, NT_DIMS, preferred_element_type=f32) * scale + bias_ref[h, 0]
        s_ctx = lax.dot_general(q, kc_ref[:, lanes], NT_DIMS, preferred_element_type=f32) * scale
        m = jnp.maximum(jnp.max(s_loc, axis=-1, keepdims=True), jnp.max(s_ctx, axis=-1, keepdims=True))
        p_loc = jnp.exp(s_loc - m)
        p_ctx = jnp.exp(s_ctx - m)
        denom = jnp.sum(p_loc, axis=-1, keepdims=True) + jnp.sum(p_ctx, axis=-1, keepdims=True)
        o = (jnp.dot(p_loc.astype(bf16), v, preferred_element_type=f32)
             + jnp.dot(p_ctx.astype(bf16), vc_ref[:, lanes], preferred_element_type=f32))
        o_ref[:, lanes] = (o / denom).astype(bf16)


def _na_bias_tables(rpb, rows):
    n_h, n_dr, _ = rpb.shape
    n_a = rows // NA_QROWS
    band_rows = []
    for c in range(GRID_W):
        cs = min(max(c - NA_WIN_W // 2, 0), GRID_W - NA_WIN_W)
        d0 = cs - c + NA_WIN_W - 1
        band_rows.append(jnp.pad(rpb[:, :, d0:d0 + NA_WIN_W], ((0, 0), (0, 0), (cs, GRID_W - NA_WIN_W - cs)),
                                 constant_values=NEG))
    band = jnp.stack(band_rows, axis=2)
    masked = jnp.full((n_h, GRID_W, GRID_W), NEG, f32)
    tabs = []
    for a in (0, 1, n_a - 1):
        start = min(max(NA_QROWS * a - NA_WIN_H // 2, 0), rows - NA_KROWS)
        q_blocks = []
        for qr in range(NA_QROWS):
            r_abs = NA_QROWS * a + qr
            rs = min(max(r_abs - NA_WIN_H // 2, 0), rows - NA_WIN_H)
            k_blocks = []
            for kr in range(NA_KROWS):
                k_abs = start + kr
                ok = rs <= k_abs < rs + NA_WIN_H
                k_blocks.append(band[:, k_abs - r_abs + NA_WIN_H - 1] if ok else masked)
            q_blocks.append(jnp.concatenate(k_blocks, axis=-1))
        tabs.append(jnp.concatenate(q_blocks, axis=-2))
    return jnp.stack(tabs, axis=1).astype(f32)


def _na(p_lat, p_ctx, bias, batch, l, lc):
    rows = l // GRID_W
    n_a = rows // NA_QROWS
    tq = NA_QROWS * GRID_W
    hps = NA_HEADS_PER_STEP
    width = hps * NA_HEAD_DIM
    gk, gv = COL_NK // width, COL_NV // width

    def variant(a):
        return jnp.where(a == 0, 0, jnp.where(a == n_a - 1, 2, 1))

    return pl.pallas_call(
        functools.partial(_na_kernel, rows=rows),
        out_shape=jax.ShapeDtypeStruct((batch * l, NA_WIDTH), bf16),
        grid=(batch, NA_HEADS // hps, n_a),
        in_specs=[pl.BlockSpec((tq, width), lambda b, g, a: (b * n_a + a, g)),
                  pl.BlockSpec((l, width), lambda b, g, a: (b, gk + g)),
                  pl.BlockSpec((l, width), lambda b, g, a: (b, gv + g)),
                  pl.BlockSpec((lc, width), lambda b, g, a: (b, gk + g)),
                  pl.BlockSpec((lc, width), lambda b, g, a: (b, gv + g)),
                  pl.BlockSpec((hps, 1, tq, NA_KROWS * GRID_W), lambda b, g, a: (g, variant(a), 0, 0))],
        out_specs=pl.BlockSpec((tq, width), lambda b, g, a: (b * n_a + a, g)),
        compiler_params=_cparams(("arbitrary", "arbitrary", "arbitrary")),
        name="na",
    )(p_lat, p_lat, p_lat, p_ctx, p_ctx, bias)


def _log_decay(dd, up, bias):
    z = jnp.dot(dd.astype(bf16), up, preferred_element_type=f32) + bias
    return (jnp.minimum(z, 0.0) - jnp.log1p(jnp.exp(-jnp.abs(z)))) * (1.0 / GLA_GATE_TAU)


def _gla_chunk(q, k, v, g, st_ref, *, rev, need_out):
    c, sub = GLA_CHUNK, GLA_SUB
    r = lax.broadcasted_iota(i32, (c, c), 0)
    col = lax.broadcasted_iota(i32, (c, c), 1)
    tri = ((r <= col) if rev else (r >= col)).astype(f32)
    cum = jnp.dot(tri, g, precision=lax.Precision.HIGHEST, preferred_element_type=f32)
    total = cum[0:1] if rev else cum[c - 1:c]
    kd = (k * jnp.exp(total - cum)).astype(bf16)
    st = st_ref[...]
    if need_out:
        qd = (q * jnp.exp(cum)).astype(bf16)
        o_inter = lax.dot_general(qd, st.astype(bf16), NT_DIMS, preferred_element_type=f32)
    st_ref[...] = st * jnp.exp(total) + lax.dot_general(v, kd, TN_DIMS, preferred_element_type=f32)
    if not need_out:
        return None
    t_io = lax.broadcasted_iota(i32, (sub, 1), 0)
    outs = []
    for i in range(c // sub):
        lo, hi = i * sub, (i + 1) * sub
        q_i, k_i, cum_i, v_i = q[lo:hi], k[lo:hi], cum[lo:hi], v[lo:hi].astype(f32)
        o_i = jnp.zeros((sub, GLA_DV), f32)
        klo, khi = (hi, c) if rev else (0, lo)
        if khi > klo:
            edge = cum[hi:hi + 1] if rev else cum[lo - 1:lo]
            qe = (q_i * jnp.exp(cum_i - edge)).astype(bf16)
            ke = (k[klo:khi] * jnp.exp(edge - cum[klo:khi])).astype(bf16)
            att = lax.dot_general(qe, ke, NT_DIMS, preferred_element_type=f32)
            o_i = o_i + jnp.dot(att.astype(bf16), v[klo:khi], preferred_element_type=f32)
        for s in range(sub):
            ok = (t_io <= s) if rev else (t_io >= s)
            e = jnp.exp(jnp.where(ok, cum_i - cum_i[s:s + 1], NEG))
            a_col = jnp.sum(q_i * k_i[s:s + 1] * e, axis=-1, keepdims=True)
            o_i = o_i + a_col * v_i[s:s + 1]
        outs.append(o_i)
    return o_inter + jnp.concatenate(outs, axis=0)


def _gla_kernel(qf_ref, kf_ref, vf_ref, ddf_ref, qb_ref, kb_ref, vb_ref, ddb_ref, kc_ref, vc_ref, ddc_ref,
                upf_ref, upb_ref, bf_ref, bb_ref, of_ref, ob_ref, sf_scr, sb_scr):
    j = pl.program_id(2)
    n_chunks = kc_ref.shape[0] // GLA_CHUNK
    upf, upb, bias_f, bias_b = upf_ref[0], upb_ref[0], bf_ref[...], bb_ref[...]

    def chunk(ci):
        return pl.ds(pl.multiple_of(ci * GLA_CHUNK, GLA_CHUNK), GLA_CHUNK)

    @pl.when(j == 0)
    def _():
        sf_scr[...] = jnp.zeros_like(sf_scr)
        sb_scr[...] = jnp.zeros_like(sb_scr)

        def ctx_chunk(ci, carry):
            cf, cb = chunk(ci), chunk(n_chunks - 1 - ci)
            gf = _log_decay(ddc_ref[cf, :], upf, bias_f)
            gb = _log_decay(ddc_ref[cb, :], upb, bias_b)
            _gla_chunk(None, kc_ref[cf, :].astype(f32), vc_ref[cf, :], gf, sf_scr, rev=False, need_out=False)
            _gla_chunk(None, kc_ref[cb, :].astype(f32), vc_ref[cb, :], gb, sb_scr, rev=True, need_out=False)
            return carry

        lax.fori_loop(0, n_chunks, ctx_chunk, 0)

    @pl.when(j > 0)
    def _():
        nc = kf_ref.shape[0] // GLA_CHUNK

        def lat_chunk(ci, carry):
            cf, cb = chunk(ci), chunk(nc - 1 - ci)
            gf = _log_decay(ddf_ref[cf, :], upf, bias_f)
            gb = _log_decay(ddb_ref[cb, :], upb, bias_b)
            o_f = _gla_chunk(qf_ref[cf, :].astype(f32), kf_ref[cf, :].astype(f32), vf_ref[cf, :], gf, sf_scr,
                             rev=False, need_out=True)
            o_b = _gla_chunk(qb_ref[cb, :].astype(f32), kb_ref[cb, :].astype(f32), vb_ref[cb, :], gb, sb_scr,
                             rev=True, need_out=True)
            of_ref[cf, :] = o_f.astype(bf16)
            ob_ref[cb, :] = o_b.astype(bf16)
            return carry

        lax.fori_loop(0, nc, lat_chunk, 0, unroll=True)


def _gla(p_lat, dd_lat, p_ctx, dd_ctx, upf, upb, bias_f, bias_b, batch, l, lc):
    tt = lc
    n_l = l // tt
    hq, hk, hv = COL_GQ // GLA_DK, COL_GK // GLA_DK, COL_GV // GLA_DV

    def fwd(b, h, j):
        return b * n_l + jnp.maximum(j - 1, 0)

    def bwd(b, h, j):
        return b * n_l + n_l - jnp.maximum(j, 1)

    lat = lambda rowmap, width, col: pl.BlockSpec((tt, width), lambda b, h, j: (rowmap(b, h, j), col + h))
    ddspec = lambda rowmap: pl.BlockSpec((tt, 128), lambda b, h, j: (rowmap(b, h, j), 0))
    ctx = lambda width, col: pl.BlockSpec((tt, width), lambda b, h, j: (b, col + h))
    return pl.pallas_call(
        _gla_kernel,
        out_shape=[jax.ShapeDtypeStruct((batch * l, GLA_VAL_WIDTH), bf16)] * 2,
        grid=(batch, GLA_HEADS, n_l + 1),
        in_specs=[lat(fwd, GLA_DK, hq), lat(fwd, GLA_DK, hk), lat(fwd, GLA_DV, hv), ddspec(fwd),
                  lat(bwd, GLA_DK, hq), lat(bwd, GLA_DK, hk), lat(bwd, GLA_DV, hv), ddspec(bwd),
                  ctx(GLA_DK, hk), ctx(GLA_DV, hv), pl.BlockSpec((tt, 128), lambda b, h, j: (b, 0)),
                  pl.BlockSpec((1, 128, GLA_DK), lambda b, h, j: (h, 0, 0)),
                  pl.BlockSpec((1, 128, GLA_DK), lambda b, h, j: (h, 0, 0)),
                  pl.BlockSpec((1, GLA_DK), lambda b, h, j: (0, h)),
                  pl.BlockSpec((1, GLA_DK), lambda b, h, j: (0, h))],
        out_specs=[pl.BlockSpec((tt, GLA_DV), lambda b, h, j: (fwd(b, h, j), h)),
                   pl.BlockSpec((tt, GLA_DV), lambda b, h, j: (bwd(b, h, j), h))],
        scratch_shapes=[pltpu.VMEM((GLA_DV, GLA_DK), f32), pltpu.VMEM((GLA_DV, GLA_DK), f32)],
        compiler_params=_cparams(("arbitrary", "arbitrary", "arbitrary")),
        name="gla",
    )(p_lat, p_lat, p_lat, dd_lat, p_lat, p_lat, p_lat, dd_lat, p_ctx, p_ctx, dd_ctx, upf, upb, bias_f, bias_b)


def _pad_gate_up(up, row0):
    w = up.reshape(GLA_GATE_RANK, GLA_HEADS, GLA_DK).transpose(1, 0, 2)
    return jnp.zeros((GLA_HEADS, 128, GLA_DK), f32).at[:, row0:row0 + GLA_GATE_RANK].set(w).astype(bf16)


def _outproj_kernel(ona_ref, of_ref, ob_ref, gate_ref, gng_ref, w_ref, x_ref, g1_ref, n2g_ref, sc2_ref, sh2_ref,
                    rwt_ref, g2_ref, ws1_ref, ws3_ref, ws2_ref, x1s_ref, h2_ref, lg_ref):
    o = of_ref[...].astype(f32) + ob_ref[...].astype(f32)
    gla = (_head_rms(o, gng_ref[...], GLA_DV) * gate_ref[...].astype(f32)).astype(bf16)
    acc = (jnp.dot(ona_ref[...], w_ref[0:NA_WIDTH, :], preferred_element_type=f32)
           + jnp.dot(gla, w_ref[NA_WIDTH:, :], preferred_element_type=f32))
    x1 = x_ref[...] + g1_ref[0] * acc
    ms = jnp.mean(x1 * x1, axis=-1, keepdims=True)
    h2 = x1 * lax.rsqrt(ms + NORM_EPS) * n2g_ref[...] * (1.0 + sc2_ref[0]) + sh2_ref[0]
    lg_ref[...] = lax.dot_general(rwt_ref[...], h2, NT_DIMS, precision=lax.Precision.HIGHEST,
                                  preferred_element_type=f32)
    h2_ref[:, 0, :] = h2
    hb = h2.astype(bf16)
    s1 = jnp.dot(hb, ws1_ref[...], preferred_element_type=f32)
    s3 = jnp.dot(hb, ws3_ref[...], preferred_element_type=f32)
    act = (s1 * jax.nn.sigmoid(s1) * s3).astype(bf16)
    x1s_ref[...] = x1 + g2_ref[0] * jnp.dot(act, ws2_ref[...], preferred_element_type=f32)


def _outproj(o_na, o_f, o_b, p_lat, gnorm_g, w_out_bf, x2d, g1, norm2_g, sc2, sh2, rwt, g2, ws1, ws3, ws2,
             tiles_per_batch, tm):
    m, d = x2d.shape
    vec = lambda i: (0, 0)
    per_b = lambda i: (i // tiles_per_batch, 0, 0)
    return pl.pallas_call(
        _outproj_kernel,
        out_shape=[jax.ShapeDtypeStruct((m, d), f32), jax.ShapeDtypeStruct((m, 1, d), f32),
                   jax.ShapeDtypeStruct((N_EXPERTS, m), f32)],
        grid=(m // tm,),
        in_specs=[pl.BlockSpec((tm, NA_WIDTH), lambda i: (i, 0)),
                  pl.BlockSpec((tm, GLA_VAL_WIDTH), lambda i: (i, 0)),
                  pl.BlockSpec((tm, GLA_VAL_WIDTH), lambda i: (i, 0)),
                  pl.BlockSpec((tm, GLA_VAL_WIDTH), lambda i: (i, COL_GATE // GLA_VAL_WIDTH)),
                  pl.BlockSpec((1, GLA_DV), vec),
                  pl.BlockSpec(w_out_bf.shape, vec),
                  pl.BlockSpec((tm, d), lambda i: (i, 0)),
                  pl.BlockSpec((1, 1, d), per_b),
                  pl.BlockSpec((1, d), vec),
                  pl.BlockSpec((1, 1, d), per_b),
                  pl.BlockSpec((1, 1, d), per_b),
                  pl.BlockSpec((N_EXPERTS, d), vec),
                  pl.BlockSpec((1, 1, d), per_b),
                  pl.BlockSpec(ws1.shape, vec),
                  pl.BlockSpec(ws3.shape, vec),
                  pl.BlockSpec(ws2.shape, vec)],
        out_specs=[pl.BlockSpec((tm, d), lambda i: (i, 0)),
                   pl.BlockSpec((tm, 1, d), lambda i: (i, 0, 0)),
                   pl.BlockSpec((N_EXPERTS, tm), lambda i: (0, i))],
        compiler_params=_cparams(("arbitrary",)),
        name="outproj",
    )(o_na, o_f, o_b, p_lat, gnorm_g, w_out_bf, x2d, g1, norm2_g, sc2, sh2, rwt, g2, ws1, ws3, ws2)


def _first_argmax(vals, iota, axis, sentinel):
    mx = jnp.max(vals, axis=axis, keepdims=True)
    am = jnp.min(jnp.where(vals == mx, iota, sentinel), axis=axis, keepdims=True)
    return mx, am


def _route_tile(lg, rb):
    t = lg.shape[1]
    per = N_EXPERTS // N_GROUPS
    scores = jax.nn.sigmoid(lg)
    biased = scores + rb
    b3 = biased.reshape(N_GROUPS, per, t)
    io3 = lax.broadcasted_iota(i32, (N_GROUPS, per, t), 1)
    m1, a1 = _first_argmax(b3, io3, 1, per)
    m2 = jnp.max(jnp.where(io3 == a1, -jnp.inf, b3), axis=1, keepdims=True)
    gs = (m1 + m2).reshape(N_GROUPS, t)
    iog = lax.broadcasted_iota(i32, (N_GROUPS, t), 0)
    gsel = jnp.zeros((N_GROUPS, t), jnp.bool_)
    for _ in range(TOPK_GROUPS):
        _, am = _first_argmax(gs, iog, 0, N_GROUPS)
        hit = iog == am
        gsel = gsel | hit
        gs = jnp.where(hit, -jnp.inf, gs)
    emask = jnp.broadcast_to(gsel[:, None, :], (N_GROUPS, per, t)).reshape(N_EXPERTS, t)
    masked = jnp.where(emask, biased, -jnp.inf)
    ioe = lax.broadcasted_iota(i32, (N_EXPERTS, t), 0)
    hits, sels = [], []
    for _ in range(TOP_K):
        _, am = _first_argmax(masked, ioe, 0, N_EXPERTS)
        hit = ioe == am
        hits.append(hit)
        sels.append(jnp.sum(jnp.where(hit, scores, 0.0), axis=0, keepdims=True))
        masked = jnp.where(hit, -jnp.inf, masked)
    denom = sels[0]
    for s in sels[1:]:
        denom = denom + s
    wts = [s / denom * ROUTED_SCALE for s in sels]
    return hits, wts


def _route_kernel(lg_ref, rb_ref, pos_ref, wts_ref, meta_ref, blk_ref, cnt_scr, carry_scr, base_scr, tri_scr):
    ph = pl.program_id(0)
    i = pl.program_id(1)
    t = lg_ref.shape[1]
    hits, wts = _route_tile(lg_ref[...], rb_ref[...])
    sel = hits[0]
    for h in hits[1:]:
        sel = sel | h
    self32 = sel.astype(f32)
    tile_cnt = jnp.sum(self32, axis=1, keepdims=True)

    @pl.when((ph == 0) & (i == 0))
    def _():
        cnt_scr[...] = jnp.zeros_like(cnt_scr)

    @pl.when(ph == 0)
    def _():
        cnt_scr[...] += jnp.broadcast_to(tile_cnt, cnt_scr.shape)

    @pl.when((ph == 1) & (i == 0))
    def _():
        cnt = cnt_scr[...]
        nblk = jnp.floor((cnt + (MOE_TM - 1)) * (1.0 / MOE_TM))
        r = lax.broadcasted_iota(i32, (N_EXPERTS, N_EXPERTS), 0)
        c = lax.broadcasted_iota(i32, (N_EXPERTS, N_EXPERTS), 1)
        excl = jnp.dot((c < r).astype(bf16), nblk.astype(bf16), preferred_element_type=f32)
        base_scr[...] = excl * MOE_TM
        carry_scr[...] = jnp.zeros_like(carry_scr)
        meta_ref[0] = cnt.astype(i32)
        meta_ref[1] = (excl * MOE_TM).astype(i32)
        incl = excl + nblk
        nb = blk_ref.shape[1]
        jj = lax.broadcasted_iota(i32, (N_EXPERTS, nb), 1).astype(f32)
        owner = jnp.sum((jnp.broadcast_to(incl[:, 0:1], (N_EXPERTS, nb)) <= jj).astype(i32), axis=0, keepdims=True)
        blk_ref[0:1, :] = jnp.minimum(owner, N_EXPERTS - 1)
        blk_ref[1:2, :] = jnp.broadcast_to(incl[N_EXPERTS - 1:N_EXPERTS, 0:1].astype(i32), (1, nb))
        blk_ref[2:8, :] = jnp.zeros((6, nb), i32)
        rr = lax.broadcasted_iota(i32, (t, t), 0)
        cc = lax.broadcasted_iota(i32, (t, t), 1)
        tri_scr[...] = (rr < cc).astype(bf16)

    @pl.when(ph == 1)
    def _():
        rank = jnp.dot(sel.astype(bf16), tri_scr[...], preferred_element_type=f32)
        posf = base_scr[:, 0:1] + carry_scr[:, 0:1] + rank
        for k in range(TOP_K):
            pos_ref[k:k + 1, :] = jnp.sum(jnp.where(hits[k], posf, 0.0), axis=0, keepdims=True).astype(i32)
            wts_ref[k:k + 1, :] = wts[k]
        pos_ref[TOP_K:8, :] = jnp.zeros((8 - TOP_K, t), i32)
        wts_ref[TOP_K:8, :] = jnp.zeros((8 - TOP_K, t), f32)
        carry_scr[...] += jnp.broadcast_to(tile_cnt, carry_scr.shape)


def _route(logits_t, router_bias, n_blocks):
    n_tok = logits_t.shape[1]
    tt = 1024
    n_tiles = n_tok // tt
    return pl.pallas_call(
        _route_kernel,
        out_shape=[jax.ShapeDtypeStruct((8, n_tok), i32), jax.ShapeDtypeStruct((8, n_tok), f32),
                   jax.ShapeDtypeStruct((2, N_EXPERTS, 128), i32), jax.ShapeDtypeStruct((8, n_blocks), i32)],
        grid=(2, n_tiles),
        in_specs=[pl.BlockSpec((N_EXPERTS, tt), lambda p, i: (0, i)),
                  pl.BlockSpec((N_EXPERTS, 1), lambda p, i: (0, 0))],
        out_specs=[pl.BlockSpec((8, tt), lambda p, i: (0, i * p)),
                   pl.BlockSpec((8, tt), lambda p, i: (0, i * p)),
                   pl.BlockSpec((2, N_EXPERTS, 128), lambda p, i: (0, 0, 0)),
                   pl.BlockSpec((8, n_blocks), lambda p, i: (0, 0))],
        scratch_shapes=[pltpu.VMEM((N_EXPERTS, 128), f32), pltpu.VMEM((N_EXPERTS, 128), f32),
                        pltpu.VMEM((N_EXPERTS, 128), f32), pltpu.VMEM((tt, tt), bf16)],
        compiler_params=_cparams(("arbitrary", "arbitrary")),
        name="route",
    )(logits_t, router_bias.reshape(N_EXPERTS, 1))


def _experts_kernel(blk_e_ref, nblk_ref, pos_ref, cnt_ref, base_ref, h2_ref, w1_hbm, w3_hbm, w2_hbm, ys_ref,
                    tok_smem, ord_smem, xbuf, sems, wbuf1, wbuf3, wbuf2, wsems, w1_scr, w3_scr, w2_scr):
    j = pl.program_id(0)
    n_valid = nblk_ref[0]
    n_tok = pos_ref.shape[0] // TOP_K
    n_slots = GATHER_AHEAD + 1

    def row_copy(row, slot, r):
        return pltpu.make_async_copy(h2_ref.at[tok_smem[row]], xbuf.at[slot, pl.ds(r, 1)], sems.at[slot])

    def weight_copies(expert, slot):
        return [pltpu.make_async_copy(hbm.at[expert], buf.at[slot], wsems.at[slot])
                for hbm, buf in ((w1_hbm, wbuf1), (w3_hbm, wbuf3), (w2_hbm, wbuf2))]

    def start_gather(block, slot):
        def issue(r, carry):
            row_copy(block * MOE_TM + r, slot, r).start()
            return carry

        lax.fori_loop(0, MOE_TM, issue, 0, unroll=8)

    def drain_gather(slot):
        def drain(r, carry):
            row_copy(0, slot, 0).wait()
            return carry

        lax.fori_loop(0, MOE_TM, drain, 0, unroll=8)

    @pl.when(j == 0)
    def _():
        def pad_expert(e, carry):
            cnt = cnt_ref[e]
            first = base_ref[e] + cnt

            def fill(r, c):
                tok_smem[first + r] = 0
                return c

            lax.fori_loop(0, (MOE_TM - cnt % MOE_TM) % MOE_TM, fill, 0)
            return carry

        lax.fori_loop(0, N_EXPERTS, pad_expert, 0)

        def invert(t, carry):
            for k in range(TOP_K):
                tok_smem[pos_ref[k * n_tok + t]] = t
            return carry

        lax.fori_loop(0, n_tok, invert, 0, unroll=4)
        ord_smem[0] = 0
        for c in weight_copies(blk_e_ref[0], 0):
            c.start()
        for ahead in range(GATHER_AHEAD):
            start_gather(jnp.minimum(ahead, n_valid - 1), ahead)

    e = blk_e_ref[jnp.minimum(j, n_valid - 1)]
    e_prev = blk_e_ref[jnp.maximum(j - 1, 0)]

    @pl.when((j == 0) | ((j < n_valid) & (e != e_prev)))
    def _():
        ordinal = ord_smem[0]
        slot = ordinal % 2
        for c in weight_copies(e, slot):
            c.wait()
        next_blk = (base_ref[e] + cnt_ref[e] + (MOE_TM - 1)) // MOE_TM

        @pl.when(next_blk < n_valid)
        def _():
            for c in weight_copies(blk_e_ref[jnp.minimum(next_blk, n_valid - 1)], 1 - slot):
                c.start()

        w1_scr[...] = wbuf1[slot].astype(bf16)
        w3_scr[...] = wbuf3[slot].astype(bf16)
        w2_scr[...] = wbuf2[slot].astype(bf16)
        ord_smem[0] = ordinal + 1

    @pl.when(j < n_valid)
    def _():
        slot = j % n_slots
        drain_gather(slot)
        nxt = jnp.minimum(j + GATHER_AHEAD, n_valid - 1) * MOE_TM
        nxt_slot = (j + GATHER_AHEAD) % n_slots
        for r in range(MOE_TM):
            row_copy(nxt + r, nxt_slot, r).start(priority=r % 2)
        x = xbuf[slot].astype(bf16)
        h1 = jnp.dot(x, w1_scr[...], preferred_element_type=f32)
        h3 = jnp.dot(x, w3_scr[...], preferred_element_type=f32)
        act = (h1 * jax.nn.sigmoid(h1) * h3).astype(bf16)
        ys_ref[:, 0, :] = jnp.dot(act, w2_scr[...], preferred_element_type=f32)

    @pl.when(j == n_valid - 1)
    def _():
        for ahead in range(1, GATHER_AHEAD + 1):
            drain_gather((j + ahead) % n_slots)

    @pl.when(j >= n_valid)
    def _():
        ys_ref[...] = jnp.zeros_like(ys_ref)


def _experts(blk_expert, nblk, pos_flat, cnt, base, h2, w1, w3, w2, n_blocks):
    d, de = w1.shape[1], w1.shape[2]
    n_rows = n_blocks * MOE_TM

    hbm = pl.BlockSpec(memory_space=pl.ANY)
    return pl.pallas_call(
        _experts_kernel,
        out_shape=jax.ShapeDtypeStruct((n_rows, 1, d), f32),
        grid_spec=pltpu.PrefetchScalarGridSpec(
            num_scalar_prefetch=5,
            grid=(n_blocks,),
            in_specs=[hbm, hbm, hbm, hbm],
            out_specs=pl.BlockSpec((MOE_TM, 1, d), lambda j, *_: (j, 0, 0)),
            scratch_shapes=[pltpu.SMEM((n_rows,), i32), pltpu.SMEM((1,), i32),
                            pltpu.VMEM((GATHER_AHEAD + 1, MOE_TM, d), f32),
                            pltpu.SemaphoreType.DMA((GATHER_AHEAD + 1,)),
                            pltpu.VMEM((2, d, de), f32), pltpu.VMEM((2, d, de), f32), pltpu.VMEM((2, de, d), f32),
                            pltpu.SemaphoreType.DMA((2,)),
                            pltpu.VMEM((d, de), bf16), pltpu.VMEM((d, de), bf16), pltpu.VMEM((de, d), bf16)]),
        compiler_params=_cparams(("arbitrary",)),
        name="experts",
    )(blk_expert, nblk, pos_flat, cnt, base, h2, w1, w3, w2)


def _combine_kernel(pos_ref, ys_ref, wts_ref, x1s_ref, g2_ref, o_ref, ybuf, sems):
    i = pl.program_id(0)
    n_tiles = 2 * pl.num_programs(0)
    n_tok = pos_ref.shape[0] // TOP_K
    tm = COMBINE_TM

    def row_copy(src_row, slot, k, t):
        return pltpu.make_async_copy(ys_ref.at[src_row], ybuf.at[slot, k, pl.ds(t, 1)], sems.at[slot])

    def drain_gather(slot):
        def drain(t, carry):
            for k in range(TOP_K):
                row_copy(0, slot, k, 0).wait()
            return carry

        lax.fori_loop(0, tm, drain, 0, unroll=2)

    def half_step(slot, next_tile):
        drain_gather(slot)
        for t in range(tm):
            for k in range(TOP_K):
                row_copy(pos_ref[k * n_tok + next_tile * tm + t], 1 - slot, k, t).start(priority=(t + k) % 2)
        rows = slice(slot * tm, (slot + 1) * tm)
        w = wts_ref[rows, :]
        acc = w[:, 0:1] * ybuf[slot, 0]
        for k in range(1, TOP_K):
            acc = acc + w[:, k:k + 1] * ybuf[slot, k]
        o_ref[rows, :] = x1s_ref[rows, :] + g2_ref[0] * acc

    @pl.when(i == 0)
    def _():
        def issue(t, carry):
            for k in range(TOP_K):
                row_copy(pos_ref[k * n_tok + t], 0, k, t).start()
            return carry

        lax.fori_loop(0, tm, issue, 0, unroll=2)

    half_step(0, 2 * i + 1)
    half_step(1, jnp.minimum(2 * i + 2, n_tiles - 1))

    @pl.when(i == pl.num_programs(0) - 1)
    def _():
        drain_gather(0)


def _combine(pos_flat, ys, wts_t, x1s, g2, tokens_per_batch):
    n_tok, d = x1s.shape
    tm = 2 * COMBINE_TM
    return pl.pallas_call(
        _combine_kernel,
        out_shape=jax.ShapeDtypeStruct((n_tok, d), f32),
        grid=(n_tok // tm,),
        in_specs=[pl.BlockSpec(memory_space=pltpu.SMEM),
                  pl.BlockSpec(memory_space=pl.ANY),
                  pl.BlockSpec((tm, 8), lambda i: (i, 0)),
                  pl.BlockSpec((tm, d), lambda i: (i, 0)),
                  pl.BlockSpec((1, 1, d), lambda i: (i // (tokens_per_batch // tm), 0, 0))],
        out_specs=pl.BlockSpec((tm, d), lambda i: (i, 0)),
        scratch_shapes=[pltpu.VMEM((2, TOP_K, COMBINE_TM, d), f32), pltpu.SemaphoreType.DMA((2,))],
        compiler_params=_cparams(("arbitrary",)),
        name="combine",
    )(pos_flat, ys, wts_t, x1s, g2)


def _layer(x, ctx, mod, norm1_g, norm2_g, w_in, q_norm_g, k_norm_g, na_rpb, up_f, bias_f, up_b, bias_b, gla_norm_g,
           w_out, router_w, router_bias, w1, w3, w2, ws1, ws3, ws2):
    batch, l, d = x.shape
    lc = ctx.shape[1]
    n_tok = batch * l
    sh1, sc1, g1, sh2, sc2, g2 = [mod[:batch, k * d:(k + 1) * d].reshape(batch, 1, d) for k in range(6)]
    csh1, csc1 = [mod[batch:batch + 1, k * d:(k + 1) * d].reshape(1, 1, d) for k in range(2)]

    w_bf = w_in.astype(bf16)
    wdd_bf = jnp.zeros((d, 128), f32).at[:, :2 * GLA_GATE_RANK].set(w_in[:, COL_DD:]).astype(bf16)
    qg, kg = q_norm_g.reshape(1, -1), k_norm_g.reshape(1, -1)
    tabs = _rope_tables(l)
    n1 = norm1_g.reshape(1, d)
    tm_lat = min(INPROJ_TM, l)
    p_lat, dd_lat = _inproj(x.reshape(n_tok, d), sc1, sh1, n1, w_bf, wdd_bf, qg, kg, tabs, tm=tm_lat,
                            tiles_per_group=l // tm_lat, rope=True)
    tm_ctx = min(INPROJ_TM, batch * lc)
    ctabs = tuple(t[:tm_ctx] for t in tabs)
    p_ctx, dd_ctx = _inproj(ctx.reshape(batch * lc, d), csc1, csh1, n1, w_bf, wdd_bf, qg, kg, ctabs, tm=tm_ctx,
                            tiles_per_group=batch * lc // tm_ctx, rope=False)

    bias = _na_bias_tables(na_rpb, l // GRID_W)
    o_na = _na(p_lat, p_ctx, bias, batch, l, lc)

    o_f, o_b = _gla(p_lat, dd_lat, p_ctx, dd_ctx, _pad_gate_up(up_f, 0), _pad_gate_up(up_b, GLA_GATE_RANK),
                    bias_f.reshape(1, -1), bias_b.reshape(1, -1), batch, l, lc)

    tm_out = 256
    x1s, h2, logits_t = _outproj(o_na, o_f, o_b, p_lat, gla_norm_g.reshape(1, -1), w_out.astype(bf16),
                                 x.reshape(n_tok, d), g1, norm2_g.reshape(1, d), sc2, sh2, router_w.T, g2,
                                 ws1.astype(bf16), ws3.astype(bf16), ws2.astype(bf16), l // tm_out, tm_out)

    n_blocks = n_tok * TOP_K // MOE_TM + N_EXPERTS
    pos, wts, meta, blk = _route(logits_t, router_bias, n_blocks)
    pos_flat = pos[:TOP_K].reshape(-1)
    ys = _experts(blk[0], blk[1, :1], pos_flat, meta[0, :, 0], meta[1, :, 0], h2, w1, w3, w2, n_blocks)
    out = _combine(pos_flat, ys, wts.T, x1s, g2, l)
    return out.reshape(batch, l, d)


def kernel(x, c, ctx, c_ctx, w_mod, b_mod, norm1_g, norm2_g, w_in, q_norm_g, k_norm_g, na_rpb, gla_gate_up_f,
           gla_gate_bias_f, gla_gate_up_b, gla_gate_bias_b, gla_norm_g, w_out, router_w, router_bias, expert_w1,
           expert_w3, expert_w2, shared_w1, shared_w3, shared_w2):
    depth = w_mod.shape[0]
    assert depth == 1, "context-stream outputs are only dropped for a single (last) layer"
    batch, d = c.shape
    cond8 = jnp.zeros((8, d), f32).at[:batch].set(c).at[batch].set(c_ctx)
    mod = _adaln(cond8, w_mod[0], b_mod[0])
    return _layer(x, ctx, mod, norm1_g[0], norm2_g[0], w_in[0], q_norm_g[0], k_norm_g[0], na_rpb[0],
                  gla_gate_up_f[0], gla_gate_bias_f[0], gla_gate_up_b[0], gla_gate_bias_b[0], gla_norm_g[0],
                  w_out[0], router_w[0], router_bias[0], expert_w1[0], expert_w3[0], expert_w2[0], shared_w1[0],
                  shared_w3[0], shared_w2[0])
```

```python
import functools

import jax
import jax.numpy as jnp
import numpy as np
from jax import lax
from jax.experimental import pallas as pl
from jax.experimental.pallas import tpu as pltpu

f32 = jnp.float32
bf16 = jnp.bfloat16
i32 = jnp.int32

GRID_W = 64
NORM_EPS = 1e-6
NA_HEADS = 8
NA_HEAD_DIM = 128
NA_WIN_H = 8
NA_WIN_W = 16
GLA_HEADS = 4
GLA_DK = 128
GLA_DV = 256
GLA_GATE_RANK = 16
GLA_GATE_TAU = 16.0
GLA_CHUNK = 64
GLA_SUB = 16
ROPE_BASE = 10000.0
N_EXPERTS = 64
N_GROUPS = 8
TOPK_GROUPS = 4
TOP_K = 6
D_EXPERT = 512
ROUTED_SCALE = 2.5

NA_WIDTH = NA_HEADS * NA_HEAD_DIM
GLA_KEY_WIDTH = GLA_HEADS * GLA_DK
GLA_VAL_WIDTH = GLA_HEADS * GLA_DV
COL_NQ, COL_NK, COL_NV = 0, NA_WIDTH, 2 * NA_WIDTH
COL_GQ = 3 * NA_WIDTH
COL_GK = COL_GQ + GLA_KEY_WIDTH
COL_GV = COL_GK + GLA_KEY_WIDTH
COL_GATE = COL_GV + GLA_VAL_WIDTH
COL_DD = COL_GATE + GLA_VAL_WIDTH
PROJ_TN = 512
INPROJ_TM = 512
NA_QROWS = 4
NA_KROWS = 12
NA_HEADS_PER_STEP = 4
MOE_TM = 256
COMBINE_TM = 128
EXPERT_HIDDEN_TN = 256
GATHER_AHEAD = 2
NEG = -1e30

VMEM_LIMIT = 56 * 1024 * 1024


def _cparams(sem, vmem=VMEM_LIMIT):
    return pltpu.CompilerParams(dimension_semantics=sem, vmem_limit_bytes=vmem)


NT_DIMS = (((1,), (1,)), ((), ()))
TN_DIMS = (((0,), (0,)), ((), ()))


def _mod_kernel(c_ref, w_ref, b_ref, o_ref):
    c = c_ref[...]
    s = (c * jax.nn.sigmoid(c)).astype(bf16)
    o_ref[...] = jnp.dot(s, w_ref[...].astype(bf16), preferred_element_type=f32) + b_ref[...]


def _adaln(cond8, w_mod, b_mod):
    d, n = w_mod.shape
    tn = 1024
    return pl.pallas_call(
        _mod_kernel,
        out_shape=jax.ShapeDtypeStruct((8, n), f32),
        grid=(n // tn,),
        in_specs=[pl.BlockSpec((8, d), lambda j: (0, 0)),
                  pl.BlockSpec((d, tn), lambda j: (0, j)),
                  pl.BlockSpec((1, tn), lambda j: (0, j))],
        out_specs=pl.BlockSpec((8, tn), lambda j: (0, j)),
        compiler_params=_cparams(("arbitrary",)),
        name="mod",
    )(cond8, w_mod, b_mod.reshape(1, n))


def _head_rms(a, g, width):
    parts = []
    for h in range(a.shape[-1] // width):
        ah = a[:, h * width:(h + 1) * width]
        ms = jnp.mean(ah * ah, axis=-1, keepdims=True)
        parts.append(ah * lax.rsqrt(ms + NORM_EPS) * g)
    return jnp.concatenate(parts, axis=-1)


def _rope(a, cos, sin_lo, sin_hi):
    parts = []
    for h in range(a.shape[-1] // GLA_DK):
        ah = a[:, h * GLA_DK:(h + 1) * GLA_DK]
        parts.append(ah * cos + pltpu.roll(ah, 32, axis=1) * sin_hi + pltpu.roll(ah, 96, axis=1) * sin_lo)
    return jnp.concatenate(parts, axis=-1)


def _inproj_kernel(x_ref, sc_ref, sh_ref, g_ref, w_hbm, wdd_ref, qg_ref, kg_ref, cos_ref, slo_ref, shi_ref,
                   o_ref, dd_ref, w_scr, sem, *, rope):
    @pl.when(pl.program_id(0) == 0)
    def _():
        fetch = pltpu.make_async_copy(w_hbm, w_scr, sem)
        fetch.start()
        fetch.wait()

    x = x_ref[...]
    ms = jnp.mean(x * x, axis=-1, keepdims=True)
    y = x * lax.rsqrt(ms + NORM_EPS) * g_ref[...]
    hb = (y * (1.0 + sc_ref[0]) + sh_ref[0]).astype(bf16)
    dd_ref[...] = jnp.dot(hb, wdd_ref[...], preferred_element_type=f32)

    def epilogue(col, acc):
        if col < COL_NK:
            return _head_rms(acc, qg_ref[...], NA_HEAD_DIM)
        if col < COL_NV:
            return _head_rms(acc, kg_ref[...], NA_HEAD_DIM)
        if col == COL_GQ:
            a = _rope(acc, cos_ref[...], slo_ref[...], shi_ref[...]) if rope else acc
            return a * (GLA_DK ** -0.5)
        if col == COL_GK:
            return _rope(acc, cos_ref[...], slo_ref[...], shi_ref[...]) if rope else acc
        if col >= COL_GATE:
            return acc * jax.nn.sigmoid(acc)
        return acc

    for col in range(0, COL_DD, PROJ_TN):
        acc = jnp.dot(hb, w_scr[:, col:col + PROJ_TN], preferred_element_type=f32)
        o_ref[:, col:col + PROJ_TN] = epilogue(col, acc).astype(bf16)


def _inproj(x2d, sc, sh, norm_g, w_bf, wdd_bf, qg, kg, rope_tabs, *, tm, tiles_per_group, rope):
    m, d = x2d.shape
    cos, slo, shi = rope_tabs
    rope_tiles = cos.shape[0] // tm
    vec = lambda i: (0, 0)
    return pl.pallas_call(
        functools.partial(_inproj_kernel, rope=rope),
        out_shape=[jax.ShapeDtypeStruct((m, COL_DD), bf16), jax.ShapeDtypeStruct((m, 128), f32)],
        grid=(m // tm,),
        in_specs=[pl.BlockSpec((tm, d), lambda i: (i, 0)),
                  pl.BlockSpec((1, 1, d), lambda i: (i // tiles_per_group, 0, 0)),
                  pl.BlockSpec((1, 1, d), lambda i: (i // tiles_per_group, 0, 0)),
                  pl.BlockSpec((1, d), vec),
                  pl.BlockSpec(memory_space=pl.ANY),
                  pl.BlockSpec((d, 128), vec),
                  pl.BlockSpec((1, NA_HEAD_DIM), vec),
                  pl.BlockSpec((1, NA_HEAD_DIM), vec),
                  pl.BlockSpec((tm, GLA_DK), lambda i: (i % rope_tiles, 0)),
                  pl.BlockSpec((tm, GLA_DK), lambda i: (i % rope_tiles, 0)),
                  pl.BlockSpec((tm, GLA_DK), lambda i: (i % rope_tiles, 0))],
        out_specs=[pl.BlockSpec((tm, COL_DD), lambda i: (i, 0)),
                   pl.BlockSpec((tm, 128), lambda i: (i, 0))],
        scratch_shapes=[pltpu.VMEM(w_bf.shape, bf16), pltpu.SemaphoreType.DMA(())],
        compiler_params=_cparams(("arbitrary",)),
        name="inproj_rope" if rope else "inproj_ctx",
    )(x2d, sc, sh, norm_g, w_bf, wdd_bf, qg, kg, cos, slo, shi)


def _rope_tables(l):
    t = np.arange(l)
    half = GLA_DK // 4
    inv_freq = ROPE_BASE ** (-np.arange(half, dtype=np.float64) / half)
    ang_r = (t // GRID_W)[:, None] * inv_freq[None, :]
    ang_c = (t % GRID_W)[:, None] * inv_freq[None, :]
    cos = np.concatenate([np.cos(ang_r)] * 2 + [np.cos(ang_c)] * 2, axis=-1)
    sin = np.concatenate([np.sin(ang_r)] * 2 + [np.sin(ang_c)] * 2, axis=-1)
    lane = np.arange(GLA_DK) % (2 * half)
    sin_lo = np.where(lane[None, :] < half, -sin, 0.0)
    sin_hi = np.where(lane[None, :] >= half, sin, 0.0)
    return tuple(jnp.asarray(a, f32) for a in (cos, sin_lo, sin_hi))


def _na_kernel(q_ref, k_ref, v_ref, kc_ref, vc_ref, bias_ref, o_ref, *, rows):
    a = pl.program_id(2)
    start = pl.multiple_of(jnp.clip(NA_QROWS * a - NA_WIN_H // 2, 0, rows - NA_KROWS) * GRID_W, GRID_W)
    scale = NA_HEAD_DIM ** -0.5
    for h in range(NA_HEADS_PER_STEP):
        lanes = slice(h * NA_HEAD_DIM, (h + 1) * NA_HEAD_DIM)
        q = q_ref[:, lanes]
        k = k_ref[pl.ds(start, NA_KROWS * GRID_W), lanes]
        v = v_ref[pl.ds(start, NA_KROWS * GRID_W), lanes]
        s_loc = lax.dot_general(q, k, NT_DIMS, preferred_element_type=f32) * scale + bias_ref[h, 0]
        s_ctx = lax.dot_general(q, kc_ref[:, lanes], NT_DIMS, preferred_element_type=f32) * scale
        m = jnp.maximum(jnp.max(s_loc, axis=-1, keepdims=True), jnp.max(s_ctx, axis=-1, keepdims=True))
        p_loc = jnp.exp(s_loc - m)
        p_ctx = jnp.exp(s_ctx - m)
        denom = jnp.sum(p_loc, axis=-1, keepdims=True) + jnp.sum(p_ctx, axis=-1, keepdims=True)
        o = (jnp.dot(p_loc.astype(bf16), v, preferred_element_type=f32)
             + jnp.dot(p_ctx.astype(bf16), vc_ref[:, lanes], preferred_element_type=f32))
        o_ref[:, lanes] = (o / denom).astype(bf16)


def _na_bias_tables(rpb, rows):
    n_h, n_dr, _ = rpb.shape
    n_a = rows // NA_QROWS
    band_rows = []
    for c in range(GRID_W):
        cs = min(max(c - NA_WIN_W // 2, 0), GRID_W - NA_WIN_W)
        d0 = cs - c + NA_WIN_W - 1
        band_rows.append(jnp.pad(rpb[:, :, d0:d0 + NA_WIN_W], ((0, 0), (0, 0), (cs, GRID_W - NA_WIN_W - cs)),
                                 constant_values=NEG))
    band = jnp.stack(band_rows, axis=2)
    masked = jnp.full((n_h, GRID_W, GRID_W), NEG, f32)
    tabs = []
    for a in (0, 1, n_a - 1):
        start = min(max(NA_QROWS * a - NA_WIN_H // 2, 0), rows - NA_KROWS)
        q_blocks = []
        for qr in range(NA_QROWS):
            r_abs = NA_QROWS * a + qr
            rs = min(max(r_abs - NA_WIN_H // 2, 0), rows - NA_WIN_H)
            k_blocks = []
            for kr in range(NA_KROWS):
                k_abs = start + kr
                ok = rs <= k_abs < rs + NA_WIN_H
                k_blocks.append(band[:, k_abs - r_abs + NA_WIN_H - 1] if ok else masked)
            q_blocks.append(jnp.concatenate(k_blocks, axis=-1))
        tabs.append(jnp.concatenate(q_blocks, axis=-2))
    return jnp.stack(tabs, axis=1).astype(f32)


def _na(p_lat, p_ctx, bias, batch, l, lc):
    rows = l // GRID_W
    n_a = rows // NA_QROWS
    tq = NA_QROWS * GRID_W
    hps = NA_HEADS_PER_STEP
    width = hps * NA_HEAD_DIM
    gk, gv = COL_NK // width, COL_NV // width

    def variant(a):
        return jnp.where(a == 0, 0, jnp.where(a == n_a - 1, 2, 1))

    return pl.pallas_call(
        functools.partial(_na_kernel, rows=rows),
        out_shape=jax.ShapeDtypeStruct((batch * l, NA_WIDTH), bf16),
        grid=(batch, NA_HEADS // hps, n_a),
        in_specs=[pl.BlockSpec((tq, width), lambda b, g, a: (b * n_a + a, g)),
                  pl.BlockSpec((l, width), lambda b, g, a: (b, gk + g)),
                  pl.BlockSpec((l, width), lambda b, g, a: (b, gv + g)),
                  pl.BlockSpec((lc, width), lambda b, g, a: (b, gk + g)),
                  pl.BlockSpec((lc, width), lambda b, g, a: (b, gv + g)),
                  pl.BlockSpec((hps, 1, tq, NA_KROWS * GRID_W), lambda b, g, a: (g, variant(a), 0, 0))],
        out_specs=pl.BlockSpec((tq, width), lambda b, g, a: (b * n_a + a, g)),
        compiler_params=_cparams(("arbitrary", "arbitrary", "arbitrary")),
        name="na",
    )(p_lat, p_lat, p_lat, p_ctx, p_ctx, bias)


def _log_decay(dd, up, bias):
    z = jnp.dot(dd.astype(bf16), up, preferred_element_type=f32) + bias
    return (jnp.minimum(z, 0.0) - jnp.log1p(jnp.exp(-jnp.abs(z)))) * (1.0 / GLA_GATE_TAU)


def _chunk_cumsums(dd, up, bias, *, rev):
    n = dd.shape[0]
    g = _log_decay(dd, up, bias)
    r = lax.broadcasted_iota(i32, (n, n), 0)
    col = lax.broadcasted_iota(i32, (n, n), 1)
    same_chunk = (r // GLA_CHUNK) == (col // GLA_CHUNK)
    tri = (same_chunk & ((r <= col) if rev else (r >= col))).astype(bf16)
    g_hi = g.astype(bf16)
    rest = g - g_hi.astype(f32)
    g_mid = rest.astype(bf16)
    g_lo = (rest - g_mid.astype(f32)).astype(bf16)
    return (jnp.dot(tri, g_hi, preferred_element_type=f32) + jnp.dot(tri, g_mid, preferred_element_type=f32)
            + jnp.dot(tri, g_lo, preferred_element_type=f32))


def _gla_chunk(q, k, v, cum, st_ref, *, rev, need_out):
    c, sub = GLA_CHUNK, GLA_SUB
    total = cum[0:1] if rev else cum[c - 1:c]
    kd = (k * jnp.exp(total - cum)).astype(bf16)
    st = st_ref[...]
    if need_out:
        qd = (q * jnp.exp(cum)).astype(bf16)
        o_inter = lax.dot_general(qd, st.astype(bf16), NT_DIMS, preferred_element_type=f32)
    st_ref[...] = st * jnp.exp(total) + lax.dot_general(v, kd, TN_DIMS, preferred_element_type=f32)
    if not need_out:
        return None
    t_io = lax.broadcasted_iota(i32, (sub, 1), 0)
    outs = []
    for i in range(c // sub):
        lo, hi = i * sub, (i + 1) * sub
        q_i, k_i, cum_i, v_i = q[lo:hi], k[lo:hi], cum[lo:hi], v[lo:hi].astype(f32)
        o_i = jnp.zeros((sub, GLA_DV), f32)
        klo, khi = (hi, c) if rev else (0, lo)
        if khi > klo:
            edge = cum[hi:hi + 1] if rev else cum[lo - 1:lo]
            qe = (q_i * jnp.exp(cum_i - edge)).astype(bf16)
            ke = (k[klo:khi] * jnp.exp(edge - cum[klo:khi])).astype(bf16)
            att = lax.dot_general(qe, ke, NT_DIMS, preferred_element_type=f32)
            o_i = o_i + jnp.dot(att.astype(bf16), v[klo:khi], preferred_element_type=f32)
        for s in range(sub):
            ok = (t_io <= s) if rev else (t_io >= s)
            e = jnp.exp(jnp.where(ok, cum_i - cum_i[s:s + 1], NEG))
            a_col = jnp.sum(q_i * k_i[s:s + 1] * e, axis=-1, keepdims=True)
            o_i = o_i + a_col * v_i[s:s + 1]
        outs.append(o_i)
    return o_inter + jnp.concatenate(outs, axis=0)


def _gla_kernel(qf_ref, kf_ref, vf_ref, ddf_ref, qb_ref, kb_ref, vb_ref, ddb_ref, kc_ref, vc_ref, ddc_ref,
                upf_ref, upb_ref, bf_ref, bb_ref, of_ref, ob_ref, sf_scr, sb_scr):
    j = pl.program_id(2)
    n_chunks = kc_ref.shape[0] // GLA_CHUNK
    upf, upb, bias_f, bias_b = upf_ref[0], upb_ref[0], bf_ref[...], bb_ref[...]

    def chunk(ci):
        return slice(ci * GLA_CHUNK, (ci + 1) * GLA_CHUNK)

    @pl.when(j == 0)
    def _():
        sf_scr[...] = jnp.zeros_like(sf_scr)
        sb_scr[...] = jnp.zeros_like(sb_scr)

        cum_f = _chunk_cumsums(ddc_ref[...], upf, bias_f, rev=False)
        cum_b = _chunk_cumsums(ddc_ref[...], upb, bias_b, rev=True)
        for ci in range(n_chunks):
            cf, cb = chunk(ci), chunk(n_chunks - 1 - ci)
            _gla_chunk(None, kc_ref[cf, :].astype(f32), vc_ref[cf, :], cum_f[cf], sf_scr, rev=False,
                       need_out=False)
            _gla_chunk(None, kc_ref[cb, :].astype(f32), vc_ref[cb, :], cum_b[cb], sb_scr, rev=True,
                       need_out=False)

    @pl.when(j > 0)
    def _():
        nc = kf_ref.shape[0] // GLA_CHUNK
        cum_f = _chunk_cumsums(ddf_ref[...], upf, bias_f, rev=False)
        cum_b = _chunk_cumsums(ddb_ref[...], upb, bias_b, rev=True)
        for ci in range(nc):
            cf, cb = chunk(ci), chunk(nc - 1 - ci)
            o_f = _gla_chunk(qf_ref[cf, :].astype(f32), kf_ref[cf, :].astype(f32), vf_ref[cf, :], cum_f[cf], sf_scr,
                             rev=False, need_out=True)
            o_b = _gla_chunk(qb_ref[cb, :].astype(f32), kb_ref[cb, :].astype(f32), vb_ref[cb, :], cum_b[cb], sb_scr,
                             rev=True, need_out=True)
            of_ref[cf, :] = o_f.astype(bf16)
            ob_ref[cb, :] = o_b.astype(bf16)


def _gla(p_lat, dd_lat, p_ctx, dd_ctx, upf, upb, bias_f, bias_b, batch, l, lc):
    tt = lc
    n_l = l // tt
    hq, hk, hv = COL_GQ // GLA_DK, COL_GK // GLA_DK, COL_GV // GLA_DV

    def fwd(b, h, j):
        return b * n_l + jnp.maximum(j - 1, 0)

    def bwd(b, h, j):
        return b * n_l + n_l - jnp.maximum(j, 1)

    lat = lambda rowmap, width, col: pl.BlockSpec((tt, width), lambda b, h, j: (rowmap(b, h, j), col + h))
    ddspec = lambda rowmap: pl.BlockSpec((tt, 128), lambda b, h, j: (rowmap(b, h, j), 0))
    ctx = lambda width, col: pl.BlockSpec((tt, width), lambda b, h, j: (b, col + h))
    return pl.pallas_call(
        _gla_kernel,
        out_shape=[jax.ShapeDtypeStruct((batch * l, GLA_VAL_WIDTH), bf16)] * 2,
        grid=(batch, GLA_HEADS, n_l + 1),
        in_specs=[lat(fwd, GLA_DK, hq), lat(fwd, GLA_DK, hk), lat(fwd, GLA_DV, hv), ddspec(fwd),
                  lat(bwd, GLA_DK, hq), lat(bwd, GLA_DK, hk), lat(bwd, GLA_DV, hv), ddspec(bwd),
                  ctx(GLA_DK, hk), ctx(GLA_DV, hv), pl.BlockSpec((tt, 128), lambda b, h, j: (b, 0)),
                  pl.BlockSpec((1, 128, GLA_DK), lambda b, h, j: (h, 0, 0)),
                  pl.BlockSpec((1, 128, GLA_DK), lambda b, h, j: (h, 0, 0)),
                  pl.BlockSpec((1, GLA_DK), lambda b, h, j: (0, h)),
                  pl.BlockSpec((1, GLA_DK), lambda b, h, j: (0, h))],
        out_specs=[pl.BlockSpec((tt, GLA_DV), lambda b, h, j: (fwd(b, h, j), h)),
                   pl.BlockSpec((tt, GLA_DV), lambda b, h, j: (bwd(b, h, j), h))],
        scratch_shapes=[pltpu.VMEM((GLA_DV, GLA_DK), f32), pltpu.VMEM((GLA_DV, GLA_DK), f32)],
        compiler_params=_cparams(("arbitrary", "arbitrary", "arbitrary")),
        name="gla",
    )(p_lat, p_lat, p_lat, dd_lat, p_lat, p_lat, p_lat, dd_lat, p_ctx, p_ctx, dd_ctx, upf, upb, bias_f, bias_b)


def _pad_gate_up(up, row0):
    w = up.reshape(GLA_GATE_RANK, GLA_HEADS, GLA_DK).transpose(1, 0, 2)
    return jnp.zeros((GLA_HEADS, 128, GLA_DK), f32).at[:, row0:row0 + GLA_GATE_RANK].set(w).astype(bf16)


def _outproj_kernel(ona_ref, of_ref, ob_ref, gate_ref, gng_ref, w_ref, x_ref, g1_ref, n2g_ref, sc2_ref, sh2_ref,
                    rwt_ref, g2_ref, ws1_ref, ws3_ref, ws2_ref, x1s_ref, h2_ref, lg_ref):
    o = of_ref[...].astype(f32) + ob_ref[...].astype(f32)
    gla = (_head_rms(o, gng_ref[...], GLA_DV) * gate_ref[...].astype(f32)).astype(bf16)
    acc = (jnp.dot(ona_ref[...], w_ref[0:NA_WIDTH, :], preferred_element_type=f32)
           + jnp.dot(gla, w_ref[NA_WIDTH:, :], preferred_element_type=f32))
    x1 = x_ref[...] + g1_ref[0] * acc
    ms = jnp.mean(x1 * x1, axis=-1, keepdims=True)
    h2 = x1 * lax.rsqrt(ms + NORM_EPS) * n2g_ref[...] * (1.0 + sc2_ref[0]) + sh2_ref[0]
    h2_ref[:, 0, :] = h2
    hb = h2.astype(bf16)
    h_lo = (h2 - hb.astype(f32)).astype(bf16)
    rw = rwt_ref[...]
    rw_hi = rw.astype(bf16)
    rw_lo = (rw - rw_hi.astype(f32)).astype(bf16)
    lg_ref[...] = (lax.dot_general(rw_hi, hb, NT_DIMS, preferred_element_type=f32)
                   + lax.dot_general(rw_hi, h_lo, NT_DIMS, preferred_element_type=f32)
                   + lax.dot_general(rw_lo, hb, NT_DIMS, preferred_element_type=f32))
    s1 = jnp.dot(hb, ws1_ref[...], preferred_element_type=f32)
    s3 = jnp.dot(hb, ws3_ref[...], preferred_element_type=f32)
    act = (s1 * jax.nn.sigmoid(s1) * s3).astype(bf16)
    x1s_ref[...] = x1 + g2_ref[0] * jnp.dot(act, ws2_ref[...], preferred_element_type=f32)


def _outproj(o_na, o_f, o_b, p_lat, gnorm_g, w_out_bf, x2d, g1, norm2_g, sc2, sh2, rwt, g2, ws1, ws3, ws2,
             tiles_per_batch, tm):
    m, d = x2d.shape
    vec = lambda i: (0, 0)
    per_b = lambda i: (i // tiles_per_batch, 0, 0)
    return pl.pallas_call(
        _outproj_kernel,
        out_shape=[jax.ShapeDtypeStruct((m, d), f32), jax.ShapeDtypeStruct((m, 1, d), f32),
                   jax.ShapeDtypeStruct((N_EXPERTS, m), f32)],
        grid=(m // tm,),
        in_specs=[pl.BlockSpec((tm, NA_WIDTH), lambda i: (i, 0)),
                  pl.BlockSpec((tm, GLA_VAL_WIDTH), lambda i: (i, 0)),
                  pl.BlockSpec((tm, GLA_VAL_WIDTH), lambda i: (i, 0)),
                  pl.BlockSpec((tm, GLA_VAL_WIDTH), lambda i: (i, COL_GATE // GLA_VAL_WIDTH)),
                  pl.BlockSpec((1, GLA_DV), vec),
                  pl.BlockSpec(w_out_bf.shape, vec),
                  pl.BlockSpec((tm, d), lambda i: (i, 0)),
                  pl.BlockSpec((1, 1, d), per_b),
                  pl.BlockSpec((1, d), vec),
                  pl.BlockSpec((1, 1, d), per_b),
                  pl.BlockSpec((1, 1, d), per_b),
                  pl.BlockSpec((N_EXPERTS, d), vec),
                  pl.BlockSpec((1, 1, d), per_b),
                  pl.BlockSpec(ws1.shape, vec),
                  pl.BlockSpec(ws3.shape, vec),
                  pl.BlockSpec(ws2.shape, vec)],
        out_specs=[pl.BlockSpec((tm, d), lambda i: (i, 0)),
                   pl.BlockSpec((tm, 1, d), lambda i: (i, 0, 0)),
                   pl.BlockSpec((N_EXPERTS, tm), lambda i: (0, i))],
        compiler_params=_cparams(("arbitrary",)),
        name="outproj",
    )(o_na, o_f, o_b, p_lat, gnorm_g, w_out_bf, x2d, g1, norm2_g, sc2, sh2, rwt, g2, ws1, ws3, ws2)


def _first_argmax(vals, iota, axis, sentinel):
    mx = jnp.max(vals, axis=axis, keepdims=True)
    am = jnp.min(jnp.where(vals == mx, iota, sentinel), axis=axis, keepdims=True)
    return mx, am


def _route_tile(lg, rb):
    t = lg.shape[1]
    per = N_EXPERTS // N_GROUPS
    scores = jax.nn.sigmoid(lg)
    biased = scores + rb
    b3 = biased.reshape(N_GROUPS, per, t)
    io3 = lax.broadcasted_iota(i32, (N_GROUPS, per, t), 1)
    m1, a1 = _first_argmax(b3, io3, 1, per)
    m2 = jnp.max(jnp.where(io3 == a1, -jnp.inf, b3), axis=1, keepdims=True)
    gs = (m1 + m2).reshape(N_GROUPS, t)
    iog = lax.broadcasted_iota(i32, (N_GROUPS, t), 0)
    gsel = jnp.zeros((N_GROUPS, t), jnp.bool_)
    for _ in range(TOPK_GROUPS):
        _, am = _first_argmax(gs, iog, 0, N_GROUPS)
        hit = iog == am
        gsel = gsel | hit
        gs = jnp.where(hit, -jnp.inf, gs)
    emask = jnp.broadcast_to(gsel[:, None, :], (N_GROUPS, per, t)).reshape(N_EXPERTS, t)
    masked = jnp.where(emask, biased, -jnp.inf)
    ioe = lax.broadcasted_iota(i32, (N_EXPERTS, t), 0)
    hits, sels = [], []
    for _ in range(TOP_K):
        _, am = _first_argmax(masked, ioe, 0, N_EXPERTS)
        hit = ioe == am
        hits.append(hit)
        sels.append(jnp.sum(jnp.where(hit, scores, 0.0), axis=0, keepdims=True))
        masked = jnp.where(hit, -jnp.inf, masked)
    denom = sels[0]
    for s in sels[1:]:
        denom = denom + s
    wts = [s / denom * ROUTED_SCALE for s in sels]
    return hits, wts


def _route_kernel(lg_ref, rb_ref, pos_ref, wts_ref, meta_ref, blk_ref, cnt_scr, carry_scr, base_scr, tri_scr):
    ph = pl.program_id(0)
    i = pl.program_id(1)
    t = lg_ref.shape[1]
    hits, wts = _route_tile(lg_ref[...], rb_ref[...])
    sel = hits[0]
    for h in hits[1:]:
        sel = sel | h
    self32 = sel.astype(f32)
    tile_cnt = jnp.sum(self32, axis=1, keepdims=True)

    @pl.when((ph == 0) & (i == 0))
    def _():
        cnt_scr[...] = jnp.zeros_like(cnt_scr)

    @pl.when(ph == 0)
    def _():
        cnt_scr[...] += jnp.broadcast_to(tile_cnt, cnt_scr.shape)

    @pl.when((ph == 1) & (i == 0))
    def _():
        cnt = cnt_scr[...]
        nblk = jnp.floor((cnt + (MOE_TM - 1)) * (1.0 / MOE_TM))
        r = lax.broadcasted_iota(i32, (N_EXPERTS, N_EXPERTS), 0)
        c = lax.broadcasted_iota(i32, (N_EXPERTS, N_EXPERTS), 1)
        excl = jnp.dot((c < r).astype(bf16), nblk.astype(bf16), preferred_element_type=f32)
        base_scr[...] = excl * MOE_TM
        carry_scr[...] = jnp.zeros_like(carry_scr)
        meta_ref[0] = cnt.astype(i32)
        meta_ref[1] = (excl * MOE_TM).astype(i32)
        incl = excl + nblk
        nb = blk_ref.shape[1]
        jj = lax.broadcasted_iota(i32, (N_EXPERTS, nb), 1).astype(f32)
        owner = jnp.sum((jnp.broadcast_to(incl[:, 0:1], (N_EXPERTS, nb)) <= jj).astype(i32), axis=0, keepdims=True)
        blk_ref[0:1, :] = jnp.minimum(owner, N_EXPERTS - 1)
        blk_ref[1:2, :] = jnp.broadcast_to(incl[N_EXPERTS - 1:N_EXPERTS, 0:1].astype(i32), (1, nb))
        blk_ref[2:8, :] = jnp.zeros((6, nb), i32)
        rr = lax.broadcasted_iota(i32, (t, t), 0)
        cc = lax.broadcasted_iota(i32, (t, t), 1)
        tri_scr[...] = (rr < cc).astype(bf16)

    @pl.when(ph == 1)
    def _():
        rank = jnp.dot(sel.astype(bf16), tri_scr[...], preferred_element_type=f32)
        posf = base_scr[:, 0:1] + carry_scr[:, 0:1] + rank
        for k in range(TOP_K):
            pos_ref[k:k + 1, :] = jnp.sum(jnp.where(hits[k], posf, 0.0), axis=0, keepdims=True).astype(i32)
            wts_ref[k:k + 1, :] = wts[k]
        pos_ref[TOP_K:8, :] = jnp.zeros((8 - TOP_K, t), i32)
        wts_ref[TOP_K:8, :] = jnp.zeros((8 - TOP_K, t), f32)
        carry_scr[...] += jnp.broadcast_to(tile_cnt, carry_scr.shape)


def _route(logits_t, router_bias, n_blocks):
    n_tok = logits_t.shape[1]
    tt = 1024
    n_tiles = n_tok // tt
    return pl.pallas_call(
        _route_kernel,
        out_shape=[jax.ShapeDtypeStruct((8, n_tok), i32), jax.ShapeDtypeStruct((8, n_tok), f32),
                   jax.ShapeDtypeStruct((2, N_EXPERTS, 128), i32), jax.ShapeDtypeStruct((8, n_blocks), i32)],
        grid=(2, n_tiles),
        in_specs=[pl.BlockSpec((N_EXPERTS, tt), lambda p, i: (0, i)),
                  pl.BlockSpec((N_EXPERTS, 1), lambda p, i: (0, 0))],
        out_specs=[pl.BlockSpec((8, tt), lambda p, i: (0, i * p)),
                   pl.BlockSpec((8, tt), lambda p, i: (0, i * p)),
                   pl.BlockSpec((2, N_EXPERTS, 128), lambda p, i: (0, 0, 0)),
                   pl.BlockSpec((8, n_blocks), lambda p, i: (0, 0))],
        scratch_shapes=[pltpu.VMEM((N_EXPERTS, 128), f32), pltpu.VMEM((N_EXPERTS, 128), f32),
                        pltpu.VMEM((N_EXPERTS, 128), f32), pltpu.VMEM((tt, tt), bf16)],
        compiler_params=_cparams(("arbitrary", "arbitrary")),
        name="route",
    )(logits_t, router_bias.reshape(N_EXPERTS, 1))


def _experts_kernel(blk_e_ref, nblk_ref, pos_ref, cnt_ref, base_ref, h2_ref, w1_hbm, w3_hbm, w2_hbm, ys_ref,
                    tok_smem, ord_smem, xbuf, sems, wbuf1, wbuf3, wbuf2, wsems, w1_scr, w3_scr, w2_scr):
    j = pl.program_id(0)
    n_valid = nblk_ref[0]
    n_tok = pos_ref.shape[0] // TOP_K
    n_slots = GATHER_AHEAD + 1

    def row_copy(row, slot, r):
        return pltpu.make_async_copy(h2_ref.at[tok_smem[row]], xbuf.at[slot, pl.ds(r, 1)], sems.at[slot])

    def weight_copies(expert, slot):
        return [pltpu.make_async_copy(hbm.at[expert], buf.at[slot], wsems.at[slot])
                for hbm, buf in ((w1_hbm, wbuf1), (w3_hbm, wbuf3), (w2_hbm, wbuf2))]

    def start_gather(block, slot):
        def issue(r, carry):
            row_copy(block * MOE_TM + r, slot, r).start()
            return carry

        lax.fori_loop(0, MOE_TM, issue, 0, unroll=8)

    def drain_gather(slot):
        def drain(r, carry):
            row_copy(0, slot, 0).wait()
            return carry

        lax.fori_loop(0, MOE_TM, drain, 0, unroll=8)

    @pl.when(j == 0)
    def _():
        ord_smem[0] = 0
        for c in weight_copies(blk_e_ref[0], 0):
            c.start()

        def pad_expert(e, carry):
            cnt = cnt_ref[e]
            first = base_ref[e] + cnt

            def fill(r, c):
                tok_smem[first + r] = 0
                return c

            lax.fori_loop(0, (MOE_TM - cnt % MOE_TM) % MOE_TM, fill, 0)
            return carry

        lax.fori_loop(0, N_EXPERTS, pad_expert, 0)

        def invert(t, carry):
            for k in range(TOP_K):
                tok_smem[pos_ref[t * TOP_K + k]] = t
            return carry

        lax.fori_loop(0, n_tok, invert, 0, unroll=8)
        for ahead in range(GATHER_AHEAD):
            start_gather(jnp.minimum(ahead, n_valid - 1), ahead)

    e = blk_e_ref[jnp.minimum(j, n_valid - 1)]
    e_prev = blk_e_ref[jnp.maximum(j - 1, 0)]

    @pl.when((j == 0) | ((j < n_valid) & (e != e_prev)))
    def _():
        ordinal = ord_smem[0]
        slot = ordinal % 2
        for c in weight_copies(e, slot):
            c.wait()
        next_blk = (base_ref[e] + cnt_ref[e] + (MOE_TM - 1)) // MOE_TM

        @pl.when(next_blk < n_valid)
        def _():
            for c in weight_copies(blk_e_ref[jnp.minimum(next_blk, n_valid - 1)], 1 - slot):
                c.start()

        w1_scr[...] = wbuf1[slot].astype(bf16)
        w3_scr[...] = wbuf3[slot].astype(bf16)
        w2_scr[...] = wbuf2[slot].astype(bf16)
        ord_smem[0] = ordinal + 1

    @pl.when(j < n_valid)
    def _():
        slot = j % n_slots
        drain_gather(slot)
        x = xbuf[slot].astype(bf16)
        nxt = jnp.minimum(j + GATHER_AHEAD, n_valid - 1) * MOE_TM
        nxt_slot = (j + GATHER_AHEAD) % n_slots
        for r in range(MOE_TM):
            row_copy(nxt + r, nxt_slot, r).start(priority=r % 2)
        y = None
        for c0 in range(0, w1_scr.shape[1], EXPERT_HIDDEN_TN):
            cols = slice(c0, c0 + EXPERT_HIDDEN_TN)
            h1 = jnp.dot(x, w1_scr[:, cols], preferred_element_type=f32)
            h3 = jnp.dot(x, w3_scr[:, cols], preferred_element_type=f32)
            act = (h1 * jax.nn.sigmoid(h1) * h3).astype(bf16)
            part = jnp.dot(act, w2_scr[cols, :], preferred_element_type=f32)
            y = part if y is None else y + part
        ys_ref[:, 0, :] = y

    @pl.when(j == n_valid - 1)
    def _():
        for ahead in range(1, GATHER_AHEAD + 1):
            drain_gather((j + ahead) % n_slots)

    @pl.when(j >= n_valid)
    def _():
        ys_ref[...] = jnp.zeros_like(ys_ref)


def _experts(blk_expert, nblk, pos_flat, cnt, base, h2, w1, w3, w2, n_blocks):
    d, de = w1.shape[1], w1.shape[2]
    n_rows = n_blocks * MOE_TM

    hbm = pl.BlockSpec(memory_space=pl.ANY)
    return pl.pallas_call(
        _experts_kernel,
        out_shape=jax.ShapeDtypeStruct((n_rows, 1, d), f32),
        grid_spec=pltpu.PrefetchScalarGridSpec(
            num_scalar_prefetch=5,
            grid=(n_blocks,),
            in_specs=[hbm, hbm, hbm, hbm],
            out_specs=pl.BlockSpec((MOE_TM, 1, d), lambda j, *_: (j, 0, 0)),
            scratch_shapes=[pltpu.SMEM((n_rows,), i32), pltpu.SMEM((1,), i32),
                            pltpu.VMEM((GATHER_AHEAD + 1, MOE_TM, d), f32),
                            pltpu.SemaphoreType.DMA((GATHER_AHEAD + 1,)),
                            pltpu.VMEM((2, d, de), f32), pltpu.VMEM((2, d, de), f32), pltpu.VMEM((2, de, d), f32),
                            pltpu.SemaphoreType.DMA((2,)),
                            pltpu.VMEM((d, de), bf16), pltpu.VMEM((d, de), bf16), pltpu.VMEM((de, d), bf16)]),
        compiler_params=_cparams(("arbitrary",)),
        name="experts",
    )(blk_expert, nblk, pos_flat, cnt, base, h2, w1, w3, w2)


def _combine_kernel(pos_ref, ys_ref, wts_ref, x1s_ref, g2_ref, o_ref, ybuf, sems):
    i = pl.program_id(0)
    n_tiles = 2 * pl.num_programs(0)
    n_tok = pos_ref.shape[0] // TOP_K
    tm = COMBINE_TM

    def row_copy(src_row, slot, k, t):
        return pltpu.make_async_copy(ys_ref.at[src_row], ybuf.at[slot, k, pl.ds(t, 1)], sems.at[slot])

    def drain_gather(slot):
        def drain(t, carry):
            for k in range(TOP_K):
                row_copy(0, slot, k, 0).wait()
            return carry

        lax.fori_loop(0, tm, drain, 0, unroll=2)

    def half_step(slot, next_tile):
        drain_gather(slot)
        for t in range(tm):
            for k in range(TOP_K):
                row_copy(pos_ref[(next_tile * tm + t) * TOP_K + k], 1 - slot, k, t).start(priority=(t + k) % 2)
        rows = slice(slot * tm, (slot + 1) * tm)
        w = wts_ref[rows, :]
        acc = w[:, 0:1] * ybuf[slot, 0]
        for k in range(1, TOP_K):
            acc = acc + w[:, k:k + 1] * ybuf[slot, k]
        o_ref[rows, :] = x1s_ref[rows, :] + g2_ref[0] * acc

    @pl.when(i == 0)
    def _():
        def issue(t, carry):
            for k in range(TOP_K):
                row_copy(pos_ref[t * TOP_K + k], 0, k, t).start()
            return carry

        lax.fori_loop(0, tm, issue, 0, unroll=2)

    half_step(0, 2 * i + 1)
    half_step(1, jnp.minimum(2 * i + 2, n_tiles - 1))

    @pl.when(i == pl.num_programs(0) - 1)
    def _():
        drain_gather(0)


def _combine(pos_flat, ys, wts_t, x1s, g2, tokens_per_batch):
    n_tok, d = x1s.shape
    tm = 2 * COMBINE_TM
    return pl.pallas_call(
        _combine_kernel,
        out_shape=jax.ShapeDtypeStruct((n_tok, d), f32),
        grid=(n_tok // tm,),
        in_specs=[pl.BlockSpec(memory_space=pltpu.SMEM),
                  pl.BlockSpec(memory_space=pl.ANY),
                  pl.BlockSpec((tm, 8), lambda i: (i, 0)),
                  pl.BlockSpec((tm, d), lambda i: (i, 0)),
                  pl.BlockSpec((1, 1, d), lambda i: (i // (tokens_per_batch // tm), 0, 0))],
        out_specs=pl.BlockSpec((tm, d), lambda i: (i, 0)),
        scratch_shapes=[pltpu.VMEM((2, TOP_K, COMBINE_TM, d), f32), pltpu.SemaphoreType.DMA((2,))],
        compiler_params=_cparams(("arbitrary",)),
        name="combine",
    )(pos_flat, ys, wts_t, x1s, g2)


def _layer(x, ctx, mod, norm1_g, norm2_g, w_in, q_norm_g, k_norm_g, na_rpb, up_f, bias_f, up_b, bias_b, gla_norm_g,
           w_out, router_w, router_bias, w1, w3, w2, ws1, ws3, ws2):
    batch, l, d = x.shape
    lc = ctx.shape[1]
    n_tok = batch * l
    sh1, sc1, g1, sh2, sc2, g2 = [mod[:batch, k * d:(k + 1) * d].reshape(batch, 1, d) for k in range(6)]
    csh1, csc1 = [mod[batch:batch + 1, k * d:(k + 1) * d].reshape(1, 1, d) for k in range(2)]

    w_bf = w_in.astype(bf16)
    wdd_bf = jnp.zeros((d, 128), f32).at[:, :2 * GLA_GATE_RANK].set(w_in[:, COL_DD:]).astype(bf16)
    qg, kg = q_norm_g.reshape(1, -1), k_norm_g.reshape(1, -1)
    tabs = _rope_tables(l)
    n1 = norm1_g.reshape(1, d)
    tm_lat = min(INPROJ_TM, l)
    p_lat, dd_lat = _inproj(x.reshape(n_tok, d), sc1, sh1, n1, w_bf, wdd_bf, qg, kg, tabs, tm=tm_lat,
                            tiles_per_group=l // tm_lat, rope=True)
    tm_ctx = min(INPROJ_TM, batch * lc)
    ctabs = tuple(t[:tm_ctx] for t in tabs)
    p_ctx, dd_ctx = _inproj(ctx.reshape(batch * lc, d), csc1, csh1, n1, w_bf, wdd_bf, qg, kg, ctabs, tm=tm_ctx,
                            tiles_per_group=batch * lc // tm_ctx, rope=False)

    bias = _na_bias_tables(na_rpb, l // GRID_W)
    o_na = _na(p_lat, p_ctx, bias, batch, l, lc)

    o_f, o_b = _gla(p_lat, dd_lat, p_ctx, dd_ctx, _pad_gate_up(up_f, 0), _pad_gate_up(up_b, GLA_GATE_RANK),
                    bias_f.reshape(1, -1), bias_b.reshape(1, -1), batch, l, lc)

    tm_out = 256
    x1s, h2, logits_t = _outproj(o_na, o_f, o_b, p_lat, gla_norm_g.reshape(1, -1), w_out.astype(bf16),
                                 x.reshape(n_tok, d), g1, norm2_g.reshape(1, d), sc2, sh2, router_w.T, g2,
                                 ws1.astype(bf16), ws3.astype(bf16), ws2.astype(bf16), l // tm_out, tm_out)

    n_blocks = n_tok * TOP_K // MOE_TM + N_EXPERTS
    pos, wts, meta, blk = _route(logits_t, router_bias, n_blocks)
    pos_flat = pos[:TOP_K].T.reshape(-1)
    ys = _experts(blk[0], blk[1, :1], pos_flat, meta[0, :, 0], meta[1, :, 0], h2, w1, w3, w2, n_blocks)
    out = _combine(pos_flat, ys, wts.T, x1s, g2, l)
    return out.reshape(batch, l, d)


def kernel(x, c, ctx, c_ctx, w_mod, b_mod, norm1_g, norm2_g, w_in, q_norm_g, k_norm_g, na_rpb, gla_gate_up_f,
           gla_gate_bias_f, gla_gate_up_b, gla_gate_bias_b, gla_norm_g, w_out, router_w, router_bias, expert_w1,
           expert_w3, expert_w2, shared_w1, shared_w3, shared_w2):
    depth = w_mod.shape[0]
    assert depth == 1, "context-stream outputs are only dropped for a single (last) layer"
    batch, d = c.shape
    cond8 = jnp.zeros((8, d), f32).at[:batch].set(c).at[batch].set(c_ctx)
    mod = _adaln(cond8, w_mod[0], b_mod[0])
    return _layer(x, ctx, mod, norm1_g[0], norm2_g[0], w_in[0], q_norm_g[0], k_norm_g[0], na_rpb[0],
                  gla_gate_up_f[0], gla_gate_bias_f[0], gla_gate_up_b[0], gla_gate_bias_b[0], gla_norm_g[0],
                  w_out[0], router_w[0], router_bias[0], expert_w1[0], expert_w3[0], expert_w2[0], shared_w1[0],
                  shared_w3[0], shared_w2[0])
```

```python
import functools

import jax
import jax.numpy as jnp
import numpy as np
from jax import lax
from jax.experimental import pallas as pl
from jax.experimental.pallas import tpu as pltpu

f32 = jnp.float32
bf16 = jnp.bfloat16
i32 = jnp.int32

GRID_W = 64
NORM_EPS = 1e-6
NA_HEADS = 8
NA_HEAD_DIM = 128
NA_WIN_H = 8
NA_WIN_W = 16
GLA_HEADS = 4
GLA_DK = 128
GLA_DV = 256
GLA_GATE_RANK = 16
GLA_GATE_TAU = 16.0
GLA_CHUNK = 64
GLA_SUB = 16
ROPE_BASE = 10000.0
N_EXPERTS = 64
N_GROUPS = 8
TOPK_GROUPS = 4
TOP_K = 6
ROUTED_SCALE = 2.5

NA_WIDTH = NA_HEADS * NA_HEAD_DIM
GLA_KEY_WIDTH = GLA_HEADS * GLA_DK
GLA_VAL_WIDTH = GLA_HEADS * GLA_DV
COL_NQ, COL_NK, COL_NV = 0, NA_WIDTH, 2 * NA_WIDTH
COL_GQ = 3 * NA_WIDTH
COL_GK = COL_GQ + GLA_KEY_WIDTH
COL_GV = COL_GK + GLA_KEY_WIDTH
COL_GATE = COL_GV + GLA_VAL_WIDTH
COL_DD = COL_GATE + GLA_VAL_WIDTH
PROJ_TN = 512
INPROJ_TM = 512
NA_QROWS = 4
NA_KROWS = 12
NA_HEADS_PER_STEP = 4
MOE_TM = 256
COMBINE_TM = 128
EXPERT_HIDDEN_TN = 256
GATHER_AHEAD = 2
NEG = -1e30

VMEM_LIMIT = 56 * 1024 * 1024


def _cparams(sem, vmem=VMEM_LIMIT):
    return pltpu.CompilerParams(dimension_semantics=sem, vmem_limit_bytes=vmem)


NT_DIMS = (((1,), (1,)), ((), ()))
TN_DIMS = (((0,), (0,)), ((), ()))


def _mod_kernel(c_ref, w_ref, b_ref, o_ref):
    c = c_ref[...]
    s = (c * jax.nn.sigmoid(c)).astype(bf16)
    o_ref[...] = jnp.dot(s, w_ref[...].astype(bf16), preferred_element_type=f32) + b_ref[...]


def _adaln(cond8, w_mod, b_mod):
    d, n = w_mod.shape
    tn = 1024
    return pl.pallas_call(
        _mod_kernel,
        out_shape=jax.ShapeDtypeStruct((8, n), f32),
        grid=(n // tn,),
        in_specs=[pl.BlockSpec((8, d), lambda j: (0, 0)),
                  pl.BlockSpec((d, tn), lambda j: (0, j)),
                  pl.BlockSpec((1, tn), lambda j: (0, j))],
        out_specs=pl.BlockSpec((8, tn), lambda j: (0, j)),
        compiler_params=_cparams(("arbitrary",)),
        name="mod",
    )(cond8, w_mod, b_mod.reshape(1, n))


def _head_rms(a, g, width):
    parts = []
    for h in range(a.shape[-1] // width):
        ah = a[:, h * width:(h + 1) * width]
        ms = jnp.mean(ah * ah, axis=-1, keepdims=True)
        parts.append(ah * lax.rsqrt(ms + NORM_EPS) * g)
    return jnp.concatenate(parts, axis=-1)


def _rope(a, cos, sin_lo, sin_hi):
    parts = []
    for h in range(a.shape[-1] // GLA_DK):
        ah = a[:, h * GLA_DK:(h + 1) * GLA_DK]
        parts.append(ah * cos + pltpu.roll(ah, 32, axis=1) * sin_hi + pltpu.roll(ah, 96, axis=1) * sin_lo)
    return jnp.concatenate(parts, axis=-1)


def _inproj_kernel(x_ref, sc_ref, sh_ref, g_ref, w_hbm, wdd_ref, qg_ref, kg_ref, cos_ref, slo_ref, shi_ref,
                   o_ref, dd_ref, w_scr, sem, *, rope):
    @pl.when(pl.program_id(0) == 0)
    def _():
        fetch = pltpu.make_async_copy(w_hbm, w_scr, sem)
        fetch.start()
        fetch.wait()

    x = x_ref[...]
    ms = jnp.mean(x * x, axis=-1, keepdims=True)
    y = x * lax.rsqrt(ms + NORM_EPS) * g_ref[...]
    hb = (y * (1.0 + sc_ref[0]) + sh_ref[0]).astype(bf16)
    dd_ref[...] = jnp.dot(hb, wdd_ref[...], preferred_element_type=f32)

    def epilogue(col, acc):
        if col < COL_NK:
            return _head_rms(acc, qg_ref[...], NA_HEAD_DIM) * (NA_HEAD_DIM ** -0.5)
        if col < COL_NV:
            return _head_rms(acc, kg_ref[...], NA_HEAD_DIM)
        if col == COL_GQ:
            a = _rope(acc, cos_ref[...], slo_ref[...], shi_ref[...]) if rope else acc
            return a * (GLA_DK ** -0.5)
        if col == COL_GK:
            return _rope(acc, cos_ref[...], slo_ref[...], shi_ref[...]) if rope else acc
        if col >= COL_GATE:
            return acc * jax.nn.sigmoid(acc)
        return acc

    for col in range(0, COL_DD, PROJ_TN):
        acc = jnp.dot(hb, w_scr[:, col:col + PROJ_TN], preferred_element_type=f32)
        o_ref[:, col:col + PROJ_TN] = epilogue(col, acc).astype(bf16)


def _inproj(x2d, sc, sh, norm_g, w_bf, wdd_bf, qg, kg, rope_tabs, *, tm, tiles_per_group, rope):
    m, d = x2d.shape
    cos, slo, shi = rope_tabs
    rope_tiles = cos.shape[0] // tm
    vec = lambda i: (0, 0)
    return pl.pallas_call(
        functools.partial(_inproj_kernel, rope=rope),
        out_shape=[jax.ShapeDtypeStruct((m, COL_DD), bf16), jax.ShapeDtypeStruct((m, 128), f32)],
        grid=(m // tm,),
        in_specs=[pl.BlockSpec((tm, d), lambda i: (i, 0)),
                  pl.BlockSpec((1, 1, d), lambda i: (i // tiles_per_group, 0, 0)),
                  pl.BlockSpec((1, 1, d), lambda i: (i // tiles_per_group, 0, 0)),
                  pl.BlockSpec((1, d), vec),
                  pl.BlockSpec(memory_space=pl.ANY),
                  pl.BlockSpec((d, 128), vec),
                  pl.BlockSpec((1, NA_HEAD_DIM), vec),
                  pl.BlockSpec((1, NA_HEAD_DIM), vec),
                  pl.BlockSpec((tm, GLA_DK), lambda i: (i % rope_tiles, 0)),
                  pl.BlockSpec((tm, GLA_DK), lambda i: (i % rope_tiles, 0)),
                  pl.BlockSpec((tm, GLA_DK), lambda i: (i % rope_tiles, 0))],
        out_specs=[pl.BlockSpec((tm, COL_DD), lambda i: (i, 0)),
                   pl.BlockSpec((tm, 128), lambda i: (i, 0))],
        scratch_shapes=[pltpu.VMEM(w_bf.shape, bf16), pltpu.SemaphoreType.DMA(())],
        compiler_params=_cparams(("arbitrary",)),
        name="inproj_rope" if rope else "inproj_ctx",
    )(x2d, sc, sh, norm_g, w_bf, wdd_bf, qg, kg, cos, slo, shi)


def _rope_tables(l):
    t = np.arange(l)
    half = GLA_DK // 4
    inv_freq = ROPE_BASE ** (-np.arange(half, dtype=np.float64) / half)
    ang_r = (t // GRID_W)[:, None] * inv_freq[None, :]
    ang_c = (t % GRID_W)[:, None] * inv_freq[None, :]
    cos = np.concatenate([np.cos(ang_r)] * 2 + [np.cos(ang_c)] * 2, axis=-1)
    sin = np.concatenate([np.sin(ang_r)] * 2 + [np.sin(ang_c)] * 2, axis=-1)
    lane = np.arange(GLA_DK) % (2 * half)
    sin_lo = np.where(lane[None, :] < half, -sin, 0.0)
    sin_hi = np.where(lane[None, :] >= half, sin, 0.0)
    return tuple(jnp.asarray(a, f32) for a in (cos, sin_lo, sin_hi))


def _na_kernel(q_ref, k_ref, v_ref, kc_ref, vc_ref, bias_ref, o_ref, *, rows):
    a = pl.program_id(2)
    start = pl.multiple_of(jnp.clip(NA_QROWS * a - NA_WIN_H // 2, 0, rows - NA_KROWS) * GRID_W, GRID_W)
    for h in range(NA_HEADS_PER_STEP):
        lanes = slice(h * NA_HEAD_DIM, (h + 1) * NA_HEAD_DIM)
        q = q_ref[:, lanes]
        k = k_ref[pl.ds(start, NA_KROWS * GRID_W), lanes]
        v = v_ref[pl.ds(start, NA_KROWS * GRID_W), lanes]
        s_loc = lax.dot_general(q, k, NT_DIMS, preferred_element_type=f32) + bias_ref[h, 0]
        s_ctx = lax.dot_general(q, kc_ref[:, lanes], NT_DIMS, preferred_element_type=f32)
        m = jnp.maximum(jnp.max(s_loc, axis=-1, keepdims=True), jnp.max(s_ctx, axis=-1, keepdims=True))
        p_loc = jnp.exp(s_loc - m)
        p_ctx = jnp.exp(s_ctx - m)
        denom = jnp.sum(p_loc, axis=-1, keepdims=True) + jnp.sum(p_ctx, axis=-1, keepdims=True)
        o = (jnp.dot(p_loc.astype(bf16), v, preferred_element_type=f32)
             + jnp.dot(p_ctx.astype(bf16), vc_ref[:, lanes], preferred_element_type=f32))
        o_ref[:, lanes] = (o / denom).astype(bf16)


def _na_bias_tables(rpb, rows):
    n_h = rpb.shape[0]
    n_a = rows // NA_QROWS
    band_rows = []
    for c in range(GRID_W):
        cs = min(max(c - NA_WIN_W // 2, 0), GRID_W - NA_WIN_W)
        d0 = cs - c + NA_WIN_W - 1
        band_rows.append(jnp.pad(rpb[:, :, d0:d0 + NA_WIN_W], ((0, 0), (0, 0), (cs, GRID_W - NA_WIN_W - cs)),
                                 constant_values=NEG))
    band = jnp.stack(band_rows, axis=2)
    masked = jnp.full((n_h, GRID_W, GRID_W), NEG, f32)
    tabs = []
    for a in (0, 1, n_a - 1):
        start = min(max(NA_QROWS * a - NA_WIN_H // 2, 0), rows - NA_KROWS)
        q_blocks = []
        for qr in range(NA_QROWS):
            r_abs = NA_QROWS * a + qr
            rs = min(max(r_abs - NA_WIN_H // 2, 0), rows - NA_WIN_H)
            k_blocks = []
            for kr in range(NA_KROWS):
                k_abs = start + kr
                ok = rs <= k_abs < rs + NA_WIN_H
                k_blocks.append(band[:, k_abs - r_abs + NA_WIN_H - 1] if ok else masked)
            q_blocks.append(jnp.concatenate(k_blocks, axis=-1))
        tabs.append(jnp.concatenate(q_blocks, axis=-2))
    return jnp.stack(tabs, axis=1).astype(f32)


def _na(p_lat, p_ctx, bias, batch, l, lc):
    rows = l // GRID_W
    n_a = rows // NA_QROWS
    tq = NA_QROWS * GRID_W
    hps = NA_HEADS_PER_STEP
    width = hps * NA_HEAD_DIM
    gk, gv = COL_NK // width, COL_NV // width

    def variant(a):
        return jnp.where(a == 0, 0, jnp.where(a == n_a - 1, 2, 1))

    return pl.pallas_call(
        functools.partial(_na_kernel, rows=rows),
        out_shape=jax.ShapeDtypeStruct((batch * l, NA_WIDTH), bf16),
        grid=(batch, NA_HEADS // hps, n_a),
        in_specs=[pl.BlockSpec((tq, width), lambda b, g, a: (b * n_a + a, g)),
                  pl.BlockSpec((l, width), lambda b, g, a: (b, gk + g)),
                  pl.BlockSpec((l, width), lambda b, g, a: (b, gv + g)),
                  pl.BlockSpec((lc, width), lambda b, g, a: (b, gk + g)),
                  pl.BlockSpec((lc, width), lambda b, g, a: (b, gv + g)),
                  pl.BlockSpec((hps, 1, tq, NA_KROWS * GRID_W), lambda b, g, a: (g, variant(a), 0, 0))],
        out_specs=pl.BlockSpec((tq, width), lambda b, g, a: (b * n_a + a, g)),
        compiler_params=_cparams(("arbitrary", "arbitrary", "arbitrary")),
        name="na",
    )(p_lat, p_lat, p_lat, p_ctx, p_ctx, bias)


def _log_decay(dd, up, bias):
    z = jnp.dot(dd.astype(bf16), up, preferred_element_type=f32) + bias
    return (jnp.minimum(z, 0.0) - jnp.log1p(jnp.exp(-jnp.abs(z)))) * (1.0 / GLA_GATE_TAU)


def _chunk_cumsums(dd, up, bias, *, rev):
    n = dd.shape[0]
    g = _log_decay(dd, up, bias)
    r = lax.broadcasted_iota(i32, (n, n), 0)
    col = lax.broadcasted_iota(i32, (n, n), 1)
    same_chunk = (r // GLA_CHUNK) == (col // GLA_CHUNK)
    tri = (same_chunk & ((r <= col) if rev else (r >= col))).astype(bf16)
    g_hi = g.astype(bf16)
    rest = g - g_hi.astype(f32)
    g_mid = rest.astype(bf16)
    g_lo = (rest - g_mid.astype(f32)).astype(bf16)
    return (jnp.dot(tri, g_hi, preferred_element_type=f32) + jnp.dot(tri, g_mid, preferred_element_type=f32)
            + jnp.dot(tri, g_lo, preferred_element_type=f32))


def _gla_chunk(q, k, v, cum, st_ref, *, rev, need_out):
    c, sub = GLA_CHUNK, GLA_SUB
    total = cum[0:1] if rev else cum[c - 1:c]
    kd = (k * jnp.exp(total - cum)).astype(bf16)
    st = st_ref[...]
    if need_out:
        qd = (q * jnp.exp(cum)).astype(bf16)
        o_inter = lax.dot_general(qd, st.astype(bf16), NT_DIMS, preferred_element_type=f32)
    st_ref[...] = st * jnp.exp(total) + lax.dot_general(v, kd, TN_DIMS, preferred_element_type=f32)
    if not need_out:
        return None
    t_io = lax.broadcasted_iota(i32, (sub, 1), 0)
    outs = []
    for i in range(c // sub):
        lo, hi = i * sub, (i + 1) * sub
        q_i, k_i, cum_i, v_i = q[lo:hi], k[lo:hi], cum[lo:hi], v[lo:hi].astype(f32)
        o_i = jnp.zeros((sub, GLA_DV), f32)
        klo, khi = (hi, c) if rev else (0, lo)
        if khi > klo:
            edge = cum[hi:hi + 1] if rev else cum[lo - 1:lo]
            qe = (q_i * jnp.exp(cum_i - edge)).astype(bf16)
            ke = (k[klo:khi] * jnp.exp(edge - cum[klo:khi])).astype(bf16)
            att = lax.dot_general(qe, ke, NT_DIMS, preferred_element_type=f32)
            o_i = o_i + jnp.dot(att.astype(bf16), v[klo:khi], preferred_element_type=f32)
        for s in range(sub):
            ok = (t_io <= s) if rev else (t_io >= s)
            e = jnp.exp(jnp.where(ok, cum_i - cum_i[s:s + 1], NEG))
            a_col = jnp.sum(q_i * k_i[s:s + 1] * e, axis=-1, keepdims=True)
            o_i = o_i + a_col * v_i[s:s + 1]
        outs.append(o_i)
    return o_inter + jnp.concatenate(outs, axis=0)


def _gla_kernel(qf_ref, kf_ref, vf_ref, ddf_ref, qb_ref, kb_ref, vb_ref, ddb_ref, kc_ref, vc_ref, ddc_ref,
                upf_ref, upb_ref, bf_ref, bb_ref, of_ref, ob_ref, sf_scr, sb_scr):
    j = pl.program_id(2)
    n_chunks = kc_ref.shape[0] // GLA_CHUNK
    upf, upb, bias_f, bias_b = upf_ref[0], upb_ref[0], bf_ref[...], bb_ref[...]

    def chunk(ci):
        return slice(ci * GLA_CHUNK, (ci + 1) * GLA_CHUNK)

    @pl.when(j == 0)
    def _():
        sf_scr[...] = jnp.zeros_like(sf_scr)
        sb_scr[...] = jnp.zeros_like(sb_scr)

        cum_f = _chunk_cumsums(ddc_ref[...], upf, bias_f, rev=False)
        cum_b = _chunk_cumsums(ddc_ref[...], upb, bias_b, rev=True)
        for ci in range(n_chunks):
            cf, cb = chunk(ci), chunk(n_chunks - 1 - ci)
            _gla_chunk(None, kc_ref[cf, :].astype(f32), vc_ref[cf, :], cum_f[cf], sf_scr, rev=False,
                       need_out=False)
            _gla_chunk(None, kc_ref[cb, :].astype(f32), vc_ref[cb, :], cum_b[cb], sb_scr, rev=True,
                       need_out=False)

    @pl.when(j > 0)
    def _():
        nc = kf_ref.shape[0] // GLA_CHUNK
        cum_f = _chunk_cumsums(ddf_ref[...], upf, bias_f, rev=False)
        cum_b = _chunk_cumsums(ddb_ref[...], upb, bias_b, rev=True)
        for ci in range(nc):
            cf, cb = chunk(ci), chunk(nc - 1 - ci)
            o_f = _gla_chunk(qf_ref[cf, :].astype(f32), kf_ref[cf, :].astype(f32), vf_ref[cf, :], cum_f[cf], sf_scr,
                             rev=False, need_out=True)
            o_b = _gla_chunk(qb_ref[cb, :].astype(f32), kb_ref[cb, :].astype(f32), vb_ref[cb, :], cum_b[cb], sb_scr,
                             rev=True, need_out=True)
            of_ref[cf, :] = o_f.astype(bf16)
            ob_ref[cb, :] = o_b.astype(bf16)


def _gla(p_lat, dd_lat, p_ctx, dd_ctx, upf, upb, bias_f, bias_b, batch, l, lc):
    tt = lc
    n_l = l // tt
    hq, hk, hv = COL_GQ // GLA_DK, COL_GK // GLA_DK, COL_GV // GLA_DV

    def fwd(b, h, j):
        return b * n_l + jnp.maximum(j - 1, 0)

    def bwd(b, h, j):
        return b * n_l + n_l - jnp.maximum(j, 1)

    lat = lambda rowmap, width, col: pl.BlockSpec((tt, width), lambda b, h, j: (rowmap(b, h, j), col + h))
    ddspec = lambda rowmap: pl.BlockSpec((tt, 128), lambda b, h, j: (rowmap(b, h, j), 0))
    ctx = lambda width, col: pl.BlockSpec((tt, width), lambda b, h, j: (b, col + h))
    return pl.pallas_call(
        _gla_kernel,
        out_shape=[jax.ShapeDtypeStruct((batch * l, GLA_VAL_WIDTH), bf16)] * 2,
        grid=(batch, GLA_HEADS, n_l + 1),
        in_specs=[lat(fwd, GLA_DK, hq), lat(fwd, GLA_DK, hk), lat(fwd, GLA_DV, hv), ddspec(fwd),
                  lat(bwd, GLA_DK, hq), lat(bwd, GLA_DK, hk), lat(bwd, GLA_DV, hv), ddspec(bwd),
                  ctx(GLA_DK, hk), ctx(GLA_DV, hv), pl.BlockSpec((tt, 128), lambda b, h, j: (b, 0)),
                  pl.BlockSpec((1, 128, GLA_DK), lambda b, h, j: (h, 0, 0)),
                  pl.BlockSpec((1, 128, GLA_DK), lambda b, h, j: (h, 0, 0)),
                  pl.BlockSpec((1, GLA_DK), lambda b, h, j: (0, h)),
                  pl.BlockSpec((1, GLA_DK), lambda b, h, j: (0, h))],
        out_specs=[pl.BlockSpec((tt, GLA_DV), lambda b, h, j: (fwd(b, h, j), h)),
                   pl.BlockSpec((tt, GLA_DV), lambda b, h, j: (bwd(b, h, j), h))],
        scratch_shapes=[pltpu.VMEM((GLA_DV, GLA_DK), f32), pltpu.VMEM((GLA_DV, GLA_DK), f32)],
        compiler_params=_cparams(("arbitrary", "arbitrary", "arbitrary")),
        name="gla",
    )(p_lat, p_lat, p_lat, dd_lat, p_lat, p_lat, p_lat, dd_lat, p_ctx, p_ctx, dd_ctx, upf, upb, bias_f, bias_b)


def _pad_gate_up(up, row0):
    w = up.reshape(GLA_GATE_RANK, GLA_HEADS, GLA_DK).transpose(1, 0, 2)
    return jnp.zeros((GLA_HEADS, 128, GLA_DK), f32).at[:, row0:row0 + GLA_GATE_RANK].set(w).astype(bf16)


def _outproj_kernel(ona_ref, of_ref, ob_ref, gate_ref, gng_ref, w_ref, x_ref, g1_ref, n2g_ref, sc2_ref, sh2_ref,
                    rwt_ref, g2_ref, ws1_ref, ws3_ref, ws2_ref, x1s_ref, h2_ref, lg_ref):
    o = of_ref[...].astype(f32) + ob_ref[...].astype(f32)
    gla = (_head_rms(o, gng_ref[...], GLA_DV) * gate_ref[...].astype(f32)).astype(bf16)
    acc = (jnp.dot(ona_ref[...], w_ref[0:NA_WIDTH, :], preferred_element_type=f32)
           + jnp.dot(gla, w_ref[NA_WIDTH:, :], preferred_element_type=f32))
    x1 = x_ref[...] + g1_ref[0] * acc
    ms = jnp.mean(x1 * x1, axis=-1, keepdims=True)
    h2 = x1 * lax.rsqrt(ms + NORM_EPS) * n2g_ref[...] * (1.0 + sc2_ref[0]) + sh2_ref[0]
    h2_ref[:, 0, :] = h2
    hb = h2.astype(bf16)
    h_lo = (h2 - hb.astype(f32)).astype(bf16)
    rw = rwt_ref[...]
    rw_hi = rw.astype(bf16)
    rw_lo = (rw - rw_hi.astype(f32)).astype(bf16)
    lg_ref[...] = (lax.dot_general(rw_hi, hb, NT_DIMS, preferred_element_type=f32)
                   + lax.dot_general(rw_hi, h_lo, NT_DIMS, preferred_element_type=f32)
                   + lax.dot_general(rw_lo, hb, NT_DIMS, preferred_element_type=f32))
    s1 = jnp.dot(hb, ws1_ref[...], preferred_element_type=f32)
    s3 = jnp.dot(hb, ws3_ref[...], preferred_element_type=f32)
    act = (s1 * jax.nn.sigmoid(s1) * s3).astype(bf16)
    x1s_ref[...] = x1 + g2_ref[0] * jnp.dot(act, ws2_ref[...], preferred_element_type=f32)


def _outproj(o_na, o_f, o_b, p_lat, gnorm_g, w_out_bf, x2d, g1, norm2_g, sc2, sh2, rwt, g2, ws1, ws3, ws2,
             tiles_per_batch, tm):
    m, d = x2d.shape
    vec = lambda i: (0, 0)
    per_b = lambda i: (i // tiles_per_batch, 0, 0)
    return pl.pallas_call(
        _outproj_kernel,
        out_shape=[jax.ShapeDtypeStruct((m, d), f32), jax.ShapeDtypeStruct((m, 1, d), f32),
                   jax.ShapeDtypeStruct((N_EXPERTS, m), f32)],
        grid=(m // tm,),
        in_specs=[pl.BlockSpec((tm, NA_WIDTH), lambda i: (i, 0)),
                  pl.BlockSpec((tm, GLA_VAL_WIDTH), lambda i: (i, 0)),
                  pl.BlockSpec((tm, GLA_VAL_WIDTH), lambda i: (i, 0)),
                  pl.BlockSpec((tm, GLA_VAL_WIDTH), lambda i: (i, COL_GATE // GLA_VAL_WIDTH)),
                  pl.BlockSpec((1, GLA_DV), vec),
                  pl.BlockSpec(w_out_bf.shape, vec),
                  pl.BlockSpec((tm, d), lambda i: (i, 0)),
                  pl.BlockSpec((1, 1, d), per_b),
                  pl.BlockSpec((1, d), vec),
                  pl.BlockSpec((1, 1, d), per_b),
                  pl.BlockSpec((1, 1, d), per_b),
                  pl.BlockSpec((N_EXPERTS, d), vec),
                  pl.BlockSpec((1, 1, d), per_b),
                  pl.BlockSpec(ws1.shape, vec),
                  pl.BlockSpec(ws3.shape, vec),
                  pl.BlockSpec(ws2.shape, vec)],
        out_specs=[pl.BlockSpec((tm, d), lambda i: (i, 0)),
                   pl.BlockSpec((tm, 1, d), lambda i: (i, 0, 0)),
                   pl.BlockSpec((N_EXPERTS, tm), lambda i: (0, i))],
        compiler_params=_cparams(("arbitrary",)),
        name="outproj",
    )(o_na, o_f, o_b, p_lat, gnorm_g, w_out_bf, x2d, g1, norm2_g, sc2, sh2, rwt, g2, ws1, ws3, ws2)


def _first_argmax(vals, iota, axis, sentinel):
    mx = jnp.max(vals, axis=axis, keepdims=True)
    am = jnp.min(jnp.where(vals == mx, iota, sentinel), axis=axis, keepdims=True)
    return mx, am


def _route_tile(lg, rb):
    t = lg.shape[1]
    per = N_EXPERTS // N_GROUPS
    scores = jax.nn.sigmoid(lg)
    biased = scores + rb
    b3 = biased.reshape(N_GROUPS, per, t)
    io3 = lax.broadcasted_iota(i32, (N_GROUPS, per, t), 1)
    m1, a1 = _first_argmax(b3, io3, 1, per)
    m2 = jnp.max(jnp.where(io3 == a1, -jnp.inf, b3), axis=1, keepdims=True)
    gs = (m1 + m2).reshape(N_GROUPS, t)
    iog = lax.broadcasted_iota(i32, (N_GROUPS, t), 0)
    gsel = jnp.zeros((N_GROUPS, t), jnp.bool_)
    for _ in range(TOPK_GROUPS):
        _, am = _first_argmax(gs, iog, 0, N_GROUPS)
        hit = iog == am
        gsel = gsel | hit
        gs = jnp.where(hit, -jnp.inf, gs)
    emask = jnp.broadcast_to(gsel[:, None, :], (N_GROUPS, per, t)).reshape(N_EXPERTS, t)
    masked = jnp.where(emask, biased, -jnp.inf)
    ioe = lax.broadcasted_iota(i32, (N_EXPERTS, t), 0)
    hits, sels = [], []
    for _ in range(TOP_K):
        _, am = _first_argmax(masked, ioe, 0, N_EXPERTS)
        hit = ioe == am
        hits.append(hit)
        sels.append(jnp.sum(jnp.where(hit, scores, 0.0), axis=0, keepdims=True))
        masked = jnp.where(hit, -jnp.inf, masked)
    denom = sels[0]
    for s in sels[1:]:
        denom = denom + s
    wts = [s / denom * ROUTED_SCALE for s in sels]
    return hits, wts


def _route_kernel(lg_ref, rb_ref, pos_ref, wts_ref, meta_ref, blk_ref, cnt_scr, carry_scr, base_scr, tri_scr):
    ph = pl.program_id(0)
    i = pl.program_id(1)
    t = lg_ref.shape[1]
    hits, wts = _route_tile(lg_ref[...], rb_ref[...])
    sel = hits[0]
    for h in hits[1:]:
        sel = sel | h
    self32 = sel.astype(f32)
    tile_cnt = jnp.sum(self32, axis=1, keepdims=True)

    @pl.when((ph == 0) & (i == 0))
    def _():
        cnt_scr[...] = jnp.zeros_like(cnt_scr)

    @pl.when(ph == 0)
    def _():
        cnt_scr[...] += jnp.broadcast_to(tile_cnt, cnt_scr.shape)

    @pl.when((ph == 1) & (i == 0))
    def _():
        cnt = cnt_scr[...]
        nblk = jnp.floor((cnt + (MOE_TM - 1)) * (1.0 / MOE_TM))
        r = lax.broadcasted_iota(i32, (N_EXPERTS, N_EXPERTS), 0)
        c = lax.broadcasted_iota(i32, (N_EXPERTS, N_EXPERTS), 1)
        excl = jnp.dot((c < r).astype(bf16), nblk.astype(bf16), preferred_element_type=f32)
        base_scr[...] = excl * MOE_TM
        carry_scr[...] = jnp.zeros_like(carry_scr)
        meta_ref[0] = cnt.astype(i32)
        meta_ref[1] = (excl * MOE_TM).astype(i32)
        incl = excl + nblk
        nb = blk_ref.shape[1]
        jj = lax.broadcasted_iota(i32, (N_EXPERTS, nb), 1).astype(f32)
        owner = jnp.sum((jnp.broadcast_to(incl[:, 0:1], (N_EXPERTS, nb)) <= jj).astype(i32), axis=0, keepdims=True)
        blk_ref[0:1, :] = jnp.minimum(owner, N_EXPERTS - 1)
        blk_ref[1:2, :] = jnp.broadcast_to(incl[N_EXPERTS - 1:N_EXPERTS, 0:1].astype(i32), (1, nb))
        blk_ref[2:8, :] = jnp.zeros((6, nb), i32)
        rr = lax.broadcasted_iota(i32, (t, t), 0)
        cc = lax.broadcasted_iota(i32, (t, t), 1)
        tri_scr[...] = (rr < cc).astype(bf16)

    @pl.when(ph == 1)
    def _():
        rank = jnp.dot(sel.astype(bf16), tri_scr[...], preferred_element_type=f32)
        posf = base_scr[:, 0:1] + carry_scr[:, 0:1] + rank
        for k in range(TOP_K):
            pos_ref[k:k + 1, :] = jnp.sum(jnp.where(hits[k], posf, 0.0), axis=0, keepdims=True).astype(i32)
            wts_ref[k:k + 1, :] = wts[k]
        pos_ref[TOP_K:8, :] = jnp.zeros((8 - TOP_K, t), i32)
        wts_ref[TOP_K:8, :] = jnp.zeros((8 - TOP_K, t), f32)
        carry_scr[...] += jnp.broadcast_to(tile_cnt, carry_scr.shape)


def _route(logits_t, router_bias, n_blocks):
    n_tok = logits_t.shape[1]
    tt = 1024
    n_tiles = n_tok // tt
    return pl.pallas_call(
        _route_kernel,
        out_shape=[jax.ShapeDtypeStruct((8, n_tok), i32), jax.ShapeDtypeStruct((8, n_tok), f32),
                   jax.ShapeDtypeStruct((2, N_EXPERTS, 128), i32), jax.ShapeDtypeStruct((8, n_blocks), i32)],
        grid=(2, n_tiles),
        in_specs=[pl.BlockSpec((N_EXPERTS, tt), lambda p, i: (0, i)),
                  pl.BlockSpec((N_EXPERTS, 1), lambda p, i: (0, 0))],
        out_specs=[pl.BlockSpec((8, tt), lambda p, i: (0, i * p)),
                   pl.BlockSpec((8, tt), lambda p, i: (0, i * p)),
                   pl.BlockSpec((2, N_EXPERTS, 128), lambda p, i: (0, 0, 0)),
                   pl.BlockSpec((8, n_blocks), lambda p, i: (0, 0))],
        scratch_shapes=[pltpu.VMEM((N_EXPERTS, 128), f32), pltpu.VMEM((N_EXPERTS, 128), f32),
                        pltpu.VMEM((N_EXPERTS, 128), f32), pltpu.VMEM((tt, tt), bf16)],
        compiler_params=_cparams(("arbitrary", "arbitrary")),
        name="route",
    )(logits_t, router_bias.reshape(N_EXPERTS, 1))


def _experts_kernel(blk_e_ref, nblk_ref, pos_ref, cnt_ref, base_ref, h2_ref, w1_hbm, w3_hbm, w2_hbm, ys_ref,
                    tok_smem, ord_smem, xbuf, sems, wbuf1, wbuf3, wbuf2, wsems, w1_scr, w3_scr, w2_scr):
    j = pl.program_id(0)
    n_valid = nblk_ref[0]
    n_tok = pos_ref.shape[0] // TOP_K
    n_slots = GATHER_AHEAD + 1

    def row_copy(row, slot, r):
        return pltpu.make_async_copy(h2_ref.at[tok_smem[row]], xbuf.at[slot, pl.ds(r, 1)], sems.at[slot])

    def weight_copies(expert, slot):
        return [pltpu.make_async_copy(hbm.at[expert], buf.at[slot], wsems.at[slot])
                for hbm, buf in ((w1_hbm, wbuf1), (w3_hbm, wbuf3), (w2_hbm, wbuf2))]

    def start_gather(block, slot):
        def issue(r, carry):
            row_copy(block * MOE_TM + r, slot, r).start()
            return carry

        lax.fori_loop(0, MOE_TM, issue, 0, unroll=8)

    def drain_gather(slot):
        def drain(r, carry):
            row_copy(0, slot, 0).wait()
            return carry

        lax.fori_loop(0, MOE_TM, drain, 0, unroll=8)

    @pl.when(j == 0)
    def _():
        ord_smem[0] = 0
        for c in weight_copies(blk_e_ref[0], 0):
            c.start(priority=1)

        def pad_expert(e, carry):
            cnt = cnt_ref[e]
            first = base_ref[e] + cnt

            def fill(r, c):
                tok_smem[first + r] = 0
                return c

            lax.fori_loop(0, (MOE_TM - cnt % MOE_TM) % MOE_TM, fill, 0)
            return carry

        lax.fori_loop(0, N_EXPERTS, pad_expert, 0)

        def invert(t, carry):
            for k in range(TOP_K):
                tok_smem[pos_ref[t * TOP_K + k]] = t
            return carry

        lax.fori_loop(0, n_tok, invert, 0, unroll=8)
        for ahead in range(GATHER_AHEAD):
            start_gather(jnp.minimum(ahead, n_valid - 1), ahead)

    e = blk_e_ref[jnp.minimum(j, n_valid - 1)]
    e_prev = blk_e_ref[jnp.maximum(j - 1, 0)]

    @pl.when((j == 0) | ((j < n_valid) & (e != e_prev)))
    def _():
        ordinal = ord_smem[0]
        slot = ordinal % 2
        for c in weight_copies(e, slot):
            c.wait()
        next_blk = (base_ref[e] + cnt_ref[e] + (MOE_TM - 1)) // MOE_TM

        @pl.when(next_blk < n_valid)
        def _():
            for c in weight_copies(blk_e_ref[jnp.minimum(next_blk, n_valid - 1)], 1 - slot):
                c.start(priority=1)

        w1_scr[...] = wbuf1[slot].astype(bf16)
        w3_scr[...] = wbuf3[slot].astype(bf16)
        w2_scr[...] = wbuf2[slot].astype(bf16)
        ord_smem[0] = ordinal + 1

    @pl.when(j < n_valid)
    def _():
        slot = j % n_slots
        drain_gather(slot)
        x = xbuf[slot].astype(bf16)
        nxt = jnp.minimum(j + GATHER_AHEAD, n_valid - 1) * MOE_TM
        nxt_slot = (j + GATHER_AHEAD) % n_slots
        for r in range(MOE_TM):
            row_copy(nxt + r, nxt_slot, r).start(priority=0)
        y = None
        for c0 in range(0, w1_scr.shape[1], EXPERT_HIDDEN_TN):
            cols = slice(c0, c0 + EXPERT_HIDDEN_TN)
            h1 = jnp.dot(x, w1_scr[:, cols], preferred_element_type=f32)
            h3 = jnp.dot(x, w3_scr[:, cols], preferred_element_type=f32)
            act = (h1 * jax.nn.sigmoid(h1) * h3).astype(bf16)
            part = jnp.dot(act, w2_scr[cols, :], preferred_element_type=f32)
            y = part if y is None else y + part
        ys_ref[:, 0, :] = y

    @pl.when(j == n_valid - 1)
    def _():
        for ahead in range(1, GATHER_AHEAD + 1):
            drain_gather((j + ahead) % n_slots)

    @pl.when(j >= n_valid)
    def _():
        ys_ref[...] = jnp.zeros_like(ys_ref)


def _experts(blk_expert, nblk, pos_flat, cnt, base, h2, w1, w3, w2, n_blocks):
    d, de = w1.shape[1], w1.shape[2]
    n_rows = n_blocks * MOE_TM

    hbm = pl.BlockSpec(memory_space=pl.ANY)
    return pl.pallas_call(
        _experts_kernel,
        out_shape=jax.ShapeDtypeStruct((n_rows, 1, d), f32),
        grid_spec=pltpu.PrefetchScalarGridSpec(
            num_scalar_prefetch=5,
            grid=(n_blocks,),
            in_specs=[hbm, hbm, hbm, hbm],
            out_specs=pl.BlockSpec((MOE_TM, 1, d), lambda j, *_: (j, 0, 0)),
            scratch_shapes=[pltpu.SMEM((n_rows,), i32), pltpu.SMEM((1,), i32),
                            pltpu.VMEM((GATHER_AHEAD + 1, MOE_TM, d), f32),
                            pltpu.SemaphoreType.DMA((GATHER_AHEAD + 1,)),
                            pltpu.VMEM((2, d, de), f32), pltpu.VMEM((2, d, de), f32), pltpu.VMEM((2, de, d), f32),
                            pltpu.SemaphoreType.DMA((2,)),
                            pltpu.VMEM((d, de), bf16), pltpu.VMEM((d, de), bf16), pltpu.VMEM((de, d), bf16)]),
        compiler_params=_cparams(("arbitrary",)),
        name="experts",
    )(blk_expert, nblk, pos_flat, cnt, base, h2, w1, w3, w2)


def _combine_kernel(pos_ref, ys_ref, wts_ref, x1s_ref, g2_ref, o_ref, ybuf, sems):
    i = pl.program_id(0)
    n_tiles = 2 * pl.num_programs(0)
    tm = COMBINE_TM

    def row_copy(src_row, slot, k, t):
        return pltpu.make_async_copy(ys_ref.at[src_row], ybuf.at[slot, k, pl.ds(t, 1)], sems.at[slot])

    def drain_gather(slot):
        def drain(t, carry):
            for k in range(TOP_K):
                row_copy(0, slot, k, 0).wait()
            return carry

        lax.fori_loop(0, tm, drain, 0, unroll=2)

    def half_step(slot, next_tile):
        drain_gather(slot)
        for t in range(tm):
            for k in range(TOP_K):
                row_copy(pos_ref[(next_tile * tm + t) * TOP_K + k], 1 - slot, k, t).start(priority=(t + k) % 2)
        rows = slice(slot * tm, (slot + 1) * tm)
        w = wts_ref[rows, :]
        acc = w[:, 0:1] * ybuf[slot, 0]
        for k in range(1, TOP_K):
            acc = acc + w[:, k:k + 1] * ybuf[slot, k]
        o_ref[rows, :] = x1s_ref[rows, :] + g2_ref[0] * acc

    @pl.when(i == 0)
    def _():
        def issue(t, carry):
            for k in range(TOP_K):
                row_copy(pos_ref[t * TOP_K + k], 0, k, t).start()
            return carry

        lax.fori_loop(0, tm, issue, 0, unroll=2)

    half_step(0, 2 * i + 1)
    half_step(1, jnp.minimum(2 * i + 2, n_tiles - 1))

    @pl.when(i == pl.num_programs(0) - 1)
    def _():
        drain_gather(0)


def _combine(pos_flat, ys, wts_t, x1s, g2, tokens_per_batch):
    n_tok, d = x1s.shape
    tm = 2 * COMBINE_TM
    return pl.pallas_call(
        _combine_kernel,
        out_shape=jax.ShapeDtypeStruct((n_tok, d), f32),
        grid=(n_tok // tm,),
        in_specs=[pl.BlockSpec(memory_space=pltpu.SMEM),
                  pl.BlockSpec(memory_space=pl.ANY),
                  pl.BlockSpec((tm, 8), lambda i: (i, 0)),
                  pl.BlockSpec((tm, d), lambda i: (i, 0)),
                  pl.BlockSpec((1, 1, d), lambda i: (i // (tokens_per_batch // tm), 0, 0))],
        out_specs=pl.BlockSpec((tm, d), lambda i: (i, 0)),
        scratch_shapes=[pltpu.VMEM((2, TOP_K, COMBINE_TM, d), f32), pltpu.SemaphoreType.DMA((2,))],
        compiler_params=_cparams(("arbitrary",)),
        name="combine",
    )(pos_flat, ys, wts_t, x1s, g2)


def _layer(x, ctx, mod, norm1_g, norm2_g, w_in, q_norm_g, k_norm_g, na_rpb, up_f, bias_f, up_b, bias_b, gla_norm_g,
           w_out, router_w, router_bias, w1, w3, w2, ws1, ws3, ws2):
    batch, l, d = x.shape
    lc = ctx.shape[1]
    n_tok = batch * l
    sh1, sc1, g1, sh2, sc2, g2 = [mod[:batch, k * d:(k + 1) * d].reshape(batch, 1, d) for k in range(6)]
    csh1, csc1 = [mod[batch:batch + 1, k * d:(k + 1) * d].reshape(1, 1, d) for k in range(2)]

    w_bf = w_in.astype(bf16)
    wdd_bf = jnp.zeros((d, 128), f32).at[:, :2 * GLA_GATE_RANK].set(w_in[:, COL_DD:]).astype(bf16)
    qg, kg = q_norm_g.reshape(1, -1), k_norm_g.reshape(1, -1)
    tabs = _rope_tables(l)
    n1 = norm1_g.reshape(1, d)
    tm_lat = min(INPROJ_TM, l)
    p_lat, dd_lat = _inproj(x.reshape(n_tok, d), sc1, sh1, n1, w_bf, wdd_bf, qg, kg, tabs, tm=tm_lat,
                            tiles_per_group=l // tm_lat, rope=True)
    tm_ctx = min(INPROJ_TM, batch * lc)
    ctabs = tuple(t[:tm_ctx] for t in tabs)
    p_ctx, dd_ctx = _inproj(ctx.reshape(batch * lc, d), csc1, csh1, n1, w_bf, wdd_bf, qg, kg, ctabs, tm=tm_ctx,
                            tiles_per_group=batch * lc // tm_ctx, rope=False)

    bias = _na_bias_tables(na_rpb, l // GRID_W)
    o_na = _na(p_lat, p_ctx, bias, batch, l, lc)

    o_f, o_b = _gla(p_lat, dd_lat, p_ctx, dd_ctx, _pad_gate_up(up_f, 0), _pad_gate_up(up_b, GLA_GATE_RANK),
                    bias_f.reshape(1, -1), bias_b.reshape(1, -1), batch, l, lc)

    tm_out = 256
    x1s, h2, logits_t = _outproj(o_na, o_f, o_b, p_lat, gla_norm_g.reshape(1, -1), w_out.astype(bf16),
                                 x.reshape(n_tok, d), g1, norm2_g.reshape(1, d), sc2, sh2, router_w.T, g2,
                                 ws1.astype(bf16), ws3.astype(bf16), ws2.astype(bf16), l // tm_out, tm_out)

    n_blocks = n_tok * TOP_K // MOE_TM + N_EXPERTS
    pos, wts, meta, blk = _route(logits_t, router_bias, n_blocks)
    pos_flat = pos[:TOP_K].T.reshape(-1)
    ys = _experts(blk[0], blk[1, :1], pos_flat, meta[0, :, 0], meta[1, :, 0], h2, w1, w3, w2, n_blocks)
    out = _combine(pos_flat, ys, wts.T, x1s, g2, l)
    return out.reshape(batch, l, d)


def kernel(x, c, ctx, c_ctx, w_mod, b_mod, norm1_g, norm2_g, w_in, q_norm_g, k_norm_g, na_rpb, gla_gate_up_f,
           gla_gate_bias_f, gla_gate_up_b, gla_gate_bias_b, gla_norm_g, w_out, router_w, router_bias, expert_w1,
           expert_w3, expert_w2, shared_w1, shared_w3, shared_w2):
    depth = w_mod.shape[0]
    assert depth == 1, "context-stream outputs are only dropped for a single (last) layer"
    batch, d = c.shape
    cond8 = jnp.zeros((8, d), f32).at[:batch].set(c).at[batch].set(c_ctx)
    mod = _adaln(cond8, w_mod[0], b_mod[0])
    return _layer(x, ctx, mod, norm1_g[0], norm2_g[0], w_in[0], q_norm_g[0], k_norm_g[0], na_rpb[0],
                  gla_gate_up_f[0], gla_gate_bias_f[0], gla_gate_up_b[0], gla_gate_bias_b[0], gla_norm_g[0],
                  w_out[0], router_w[0], router_bias[0], expert_w1[0], expert_w3[0], expert_w2[0], shared_w1[0],
                  shared_w3[0], shared_w2[0])
```

```python
import functools

import jax
import jax.numpy as jnp
import numpy as np
from jax import lax
from jax.experimental import pallas as pl
from jax.experimental.pallas import tpu as pltpu

f32 = jnp.float32
bf16 = jnp.bfloat16
i32 = jnp.int32

GRID_W = 64
NORM_EPS = 1e-6
NA_HEADS = 8
NA_HEAD_DIM = 128
NA_WIN_H = 8
NA_WIN_W = 16
GLA_HEADS = 4
GLA_DK = 128
GLA_DV = 256
GLA_GATE_RANK = 16
GLA_GATE_TAU = 16.0
GLA_CHUNK = 64
GLA_SUB = 16
ROPE_BASE = 10000.0
N_EXPERTS = 64
N_GROUPS = 8
TOPK_GROUPS = 4
TOP_K = 6
ROUTED_SCALE = 2.5

NA_WIDTH = NA_HEADS * NA_HEAD_DIM
GLA_KEY_WIDTH = GLA_HEADS * GLA_DK
GLA_VAL_WIDTH = GLA_HEADS * GLA_DV
COL_NQ, COL_NK, COL_NV = 0, NA_WIDTH, 2 * NA_WIDTH
COL_GQ = 3 * NA_WIDTH
COL_GK = COL_GQ + GLA_KEY_WIDTH
COL_GV = COL_GK + GLA_KEY_WIDTH
COL_GATE = COL_GV + GLA_VAL_WIDTH
COL_DD = COL_GATE + GLA_VAL_WIDTH
PROJ_TN = 512
INPROJ_TM = 512
NA_QROWS = 4
NA_KROWS = 12
NA_HEADS_PER_STEP = 4
MOE_TM = 256
COMBINE_TM = 128
EXPERT_HIDDEN_TN = 256
GATHER_AHEAD = 2
NEG = -1e30

VMEM_LIMIT = 56 * 1024 * 1024


def _cparams(sem, vmem=VMEM_LIMIT):
    return pltpu.CompilerParams(dimension_semantics=sem, vmem_limit_bytes=vmem)


NT_DIMS = (((1,), (1,)), ((), ()))
TN_DIMS = (((0,), (0,)), ((), ()))


def _mod_kernel(c_ref, w_ref, b_ref, o_ref):
    c = c_ref[...]
    s = (c * jax.nn.sigmoid(c)).astype(bf16)
    o_ref[...] = jnp.dot(s, w_ref[...].astype(bf16), preferred_element_type=f32) + b_ref[...]


def _adaln(cond8, w_mod, b_mod):
    d, n = w_mod.shape
    tn = 1024
    return pl.pallas_call(
        _mod_kernel,
        out_shape=jax.ShapeDtypeStruct((8, n), f32),
        grid=(n // tn,),
        in_specs=[pl.BlockSpec((8, d), lambda j: (0, 0)),
                  pl.BlockSpec((d, tn), lambda j: (0, j)),
                  pl.BlockSpec((1, tn), lambda j: (0, j))],
        out_specs=pl.BlockSpec((8, tn), lambda j: (0, j)),
        compiler_params=_cparams(("arbitrary",)),
        name="mod",
    )(cond8, w_mod, b_mod.reshape(1, n))


def _head_rms(a, g, width):
    parts = []
    for h in range(a.shape[-1] // width):
        ah = a[:, h * width:(h + 1) * width]
        ms = jnp.mean(ah * ah, axis=-1, keepdims=True)
        parts.append(ah * lax.rsqrt(ms + NORM_EPS) * g)
    return jnp.concatenate(parts, axis=-1)


def _rope(a, cos, sin_lo, sin_hi):
    parts = []
    for h in range(a.shape[-1] // GLA_DK):
        ah = a[:, h * GLA_DK:(h + 1) * GLA_DK]
        parts.append(ah * cos + pltpu.roll(ah, 32, axis=1) * sin_hi + pltpu.roll(ah, 96, axis=1) * sin_lo)
    return jnp.concatenate(parts, axis=-1)


def _inproj_kernel(x_ref, sc_ref, sh_ref, g_ref, w_hbm, wdd_ref, qg_ref, kg_ref, cos_ref, slo_ref, shi_ref,
                   o_ref, dd_ref, w_scr, sem, *, rope):
    @pl.when(pl.program_id(0) == 0)
    def _():
        fetch = pltpu.make_async_copy(w_hbm, w_scr, sem)
        fetch.start()
        fetch.wait()

    x = x_ref[...]
    ms = jnp.mean(x * x, axis=-1, keepdims=True)
    y = x * lax.rsqrt(ms + NORM_EPS) * g_ref[...]
    hb = (y * (1.0 + sc_ref[0]) + sh_ref[0]).astype(bf16)
    dd_ref[...] = jnp.dot(hb, wdd_ref[...], preferred_element_type=f32)

    def epilogue(col, acc):
        if col < COL_NK:
            return _head_rms(acc, qg_ref[...], NA_HEAD_DIM) * (NA_HEAD_DIM ** -0.5)
        if col < COL_NV:
            return _head_rms(acc, kg_ref[...], NA_HEAD_DIM)
        if col == COL_GQ:
            a = _rope(acc, cos_ref[...], slo_ref[...], shi_ref[...]) if rope else acc
            return a * (GLA_DK ** -0.5)
        if col == COL_GK:
            return _rope(acc, cos_ref[...], slo_ref[...], shi_ref[...]) if rope else acc
        if col >= COL_GATE:
            return acc * jax.nn.sigmoid(acc)
        return acc

    for col in range(0, COL_DD, PROJ_TN):
        acc = jnp.dot(hb, w_scr[:, col:col + PROJ_TN], preferred_element_type=f32)
        o_ref[:, col:col + PROJ_TN] = epilogue(col, acc).astype(bf16)


def _inproj(x2d, sc, sh, norm_g, w_bf, wdd_bf, qg, kg, rope_tabs, *, tm, tiles_per_group, rope):
    m, d = x2d.shape
    cos, slo, shi = rope_tabs
    rope_tiles = cos.shape[0] // tm
    vec = lambda i: (0, 0)
    return pl.pallas_call(
        functools.partial(_inproj_kernel, rope=rope),
        out_shape=[jax.ShapeDtypeStruct((m, COL_DD), bf16), jax.ShapeDtypeStruct((m, 128), f32)],
        grid=(m // tm,),
        in_specs=[pl.BlockSpec((tm, d), lambda i: (i, 0)),
                  pl.BlockSpec((1, 1, d), lambda i: (i // tiles_per_group, 0, 0)),
                  pl.BlockSpec((1, 1, d), lambda i: (i // tiles_per_group, 0, 0)),
                  pl.BlockSpec((1, d), vec),
                  pl.BlockSpec(memory_space=pl.ANY),
                  pl.BlockSpec((d, 128), vec),
                  pl.BlockSpec((1, NA_HEAD_DIM), vec),
                  pl.BlockSpec((1, NA_HEAD_DIM), vec),
                  pl.BlockSpec((tm, GLA_DK), lambda i: (i % rope_tiles, 0)),
                  pl.BlockSpec((tm, GLA_DK), lambda i: (i % rope_tiles, 0)),
                  pl.BlockSpec((tm, GLA_DK), lambda i: (i % rope_tiles, 0))],
        out_specs=[pl.BlockSpec((tm, COL_DD), lambda i: (i, 0)),
                   pl.BlockSpec((tm, 128), lambda i: (i, 0))],
        scratch_shapes=[pltpu.VMEM(w_bf.shape, bf16), pltpu.SemaphoreType.DMA(())],
        compiler_params=_cparams(("arbitrary",)),
        name="inproj_rope" if rope else "inproj_ctx",
    )(x2d, sc, sh, norm_g, w_bf, wdd_bf, qg, kg, cos, slo, shi)


def _rope_tables(l):
    t = np.arange(l)
    half = GLA_DK // 4
    inv_freq = ROPE_BASE ** (-np.arange(half, dtype=np.float64) / half)
    ang_r = (t // GRID_W)[:, None] * inv_freq[None, :]
    ang_c = (t % GRID_W)[:, None] * inv_freq[None, :]
    cos = np.concatenate([np.cos(ang_r)] * 2 + [np.cos(ang_c)] * 2, axis=-1)
    sin = np.concatenate([np.sin(ang_r)] * 2 + [np.sin(ang_c)] * 2, axis=-1)
    lane = np.arange(GLA_DK) % (2 * half)
    sin_lo = np.where(lane[None, :] < half, -sin, 0.0)
    sin_hi = np.where(lane[None, :] >= half, sin, 0.0)
    return tuple(jnp.asarray(a, f32) for a in (cos, sin_lo, sin_hi))


def _na_kernel(q_ref, k_ref, v_ref, kc_ref, vc_ref, bias_ref, o_ref, *, rows):
    a = pl.program_id(2)
    start = pl.multiple_of(jnp.clip(NA_QROWS * a - NA_WIN_H // 2, 0, rows - NA_KROWS) * GRID_W, GRID_W)
    for h in range(NA_HEADS_PER_STEP):
        lanes = slice(h * NA_HEAD_DIM, (h + 1) * NA_HEAD_DIM)
        q = q_ref[:, lanes]
        k = k_ref[pl.ds(start, NA_KROWS * GRID_W), lanes]
        v = v_ref[pl.ds(start, NA_KROWS * GRID_W), lanes]
        s_loc = lax.dot_general(q, k, NT_DIMS, preferred_element_type=f32) + bias_ref[h, 0]
        s_ctx = lax.dot_general(q, kc_ref[:, lanes], NT_DIMS, preferred_element_type=f32)
        m = jnp.maximum(jnp.max(s_loc, axis=-1, keepdims=True), jnp.max(s_ctx, axis=-1, keepdims=True))
        p_loc = jnp.exp(s_loc - m)
        p_ctx = jnp.exp(s_ctx - m)
        denom = jnp.sum(p_loc, axis=-1, keepdims=True) + jnp.sum(p_ctx, axis=-1, keepdims=True)
        o = (jnp.dot(p_loc.astype(bf16), v, preferred_element_type=f32)
             + jnp.dot(p_ctx.astype(bf16), vc_ref[:, lanes], preferred_element_type=f32))
        o_ref[:, lanes] = (o / denom).astype(bf16)


def _na_bias_tables(rpb, rows):
    n_h = rpb.shape[0]
    n_a = rows // NA_QROWS
    band_rows = []
    for c in range(GRID_W):
        cs = min(max(c - NA_WIN_W // 2, 0), GRID_W - NA_WIN_W)
        d0 = cs - c + NA_WIN_W - 1
        band_rows.append(jnp.pad(rpb[:, :, d0:d0 + NA_WIN_W], ((0, 0), (0, 0), (cs, GRID_W - NA_WIN_W - cs)),
                                 constant_values=NEG))
    band = jnp.stack(band_rows, axis=2)
    masked = jnp.full((n_h, GRID_W, GRID_W), NEG, f32)
    tabs = []
    for a in (0, 1, n_a - 1):
        start = min(max(NA_QROWS * a - NA_WIN_H // 2, 0), rows - NA_KROWS)
        q_blocks = []
        for qr in range(NA_QROWS):
            r_abs = NA_QROWS * a + qr
            rs = min(max(r_abs - NA_WIN_H // 2, 0), rows - NA_WIN_H)
            k_blocks = []
            for kr in range(NA_KROWS):
                k_abs = start + kr
                ok = rs <= k_abs < rs + NA_WIN_H
                k_blocks.append(band[:, k_abs - r_abs + NA_WIN_H - 1] if ok else masked)
            q_blocks.append(jnp.concatenate(k_blocks, axis=-1))
        tabs.append(jnp.concatenate(q_blocks, axis=-2))
    return jnp.stack(tabs, axis=1).astype(f32)


def _na(p_lat, p_ctx, bias, batch, l, lc):
    rows = l // GRID_W
    n_a = rows // NA_QROWS
    tq = NA_QROWS * GRID_W
    hps = NA_HEADS_PER_STEP
    width = hps * NA_HEAD_DIM
    gk, gv = COL_NK // width, COL_NV // width

    def variant(a):
        return jnp.where(a == 0, 0, jnp.where(a == n_a - 1, 2, 1))

    return pl.pallas_call(
        functools.partial(_na_kernel, rows=rows),
        out_shape=jax.ShapeDtypeStruct((batch * l, NA_WIDTH), bf16),
        grid=(batch, NA_HEADS // hps, n_a),
        in_specs=[pl.BlockSpec((tq, width), lambda b, g, a: (b * n_a + a, g)),
                  pl.BlockSpec((l, width), lambda b, g, a: (b, gk + g)),
                  pl.BlockSpec((l, width), lambda b, g, a: (b, gv + g)),
                  pl.BlockSpec((lc, width), lambda b, g, a: (b, gk + g)),
                  pl.BlockSpec((lc, width), lambda b, g, a: (b, gv + g)),
                  pl.BlockSpec((hps, 1, tq, NA_KROWS * GRID_W), lambda b, g, a: (g, variant(a), 0, 0))],
        out_specs=pl.BlockSpec((tq, width), lambda b, g, a: (b * n_a + a, g)),
        compiler_params=_cparams(("arbitrary", "arbitrary", "arbitrary")),
        name="na",
    )(p_lat, p_lat, p_lat, p_ctx, p_ctx, bias)


def _log_decay(dd, up, bias):
    z = jnp.dot(dd.astype(bf16), up, preferred_element_type=f32) + bias
    return (jnp.minimum(z, 0.0) - jnp.log1p(jnp.exp(-jnp.abs(z)))) * (1.0 / GLA_GATE_TAU)


def _chunk_cumsums(dd, up, bias, *, rev):
    n = dd.shape[0]
    g = _log_decay(dd, up, bias)
    r = lax.broadcasted_iota(i32, (n, n), 0)
    col = lax.broadcasted_iota(i32, (n, n), 1)
    same_chunk = (r // GLA_CHUNK) == (col // GLA_CHUNK)
    tri = (same_chunk & ((r <= col) if rev else (r >= col))).astype(bf16)
    g_hi = g.astype(bf16)
    rest = g - g_hi.astype(f32)
    g_mid = rest.astype(bf16)
    g_lo = (rest - g_mid.astype(f32)).astype(bf16)
    return (jnp.dot(tri, g_hi, preferred_element_type=f32) + jnp.dot(tri, g_mid, preferred_element_type=f32)
            + jnp.dot(tri, g_lo, preferred_element_type=f32))


def _gla_chunk(q, k, v, cum, st_ref, *, rev, need_out):
    c, sub = GLA_CHUNK, GLA_SUB
    total = cum[0:1] if rev else cum[c - 1:c]
    kd = (k * jnp.exp(total - cum)).astype(bf16)
    st = st_ref[...]
    if need_out:
        qd = (q * jnp.exp(cum)).astype(bf16)
        o_inter = lax.dot_general(qd, st.astype(bf16), NT_DIMS, preferred_element_type=f32)
    st_ref[...] = st * jnp.exp(total) + lax.dot_general(v, kd, TN_DIMS, preferred_element_type=f32)
    if not need_out:
        return None
    t_io = lax.broadcasted_iota(i32, (sub, 1), 0)
    outs = []
    for i in range(c // sub):
        lo, hi = i * sub, (i + 1) * sub
        q_i, k_i, cum_i, v_i = q[lo:hi], k[lo:hi], cum[lo:hi], v[lo:hi].astype(f32)
        o_i = jnp.zeros((sub, GLA_DV), f32)
        klo, khi = (hi, c) if rev else (0, lo)
        if khi > klo:
            edge = cum[hi:hi + 1] if rev else cum[lo - 1:lo]
            qe = (q_i * jnp.exp(cum_i - edge)).astype(bf16)
            ke = (k[klo:khi] * jnp.exp(edge - cum[klo:khi])).astype(bf16)
            att = lax.dot_general(qe, ke, NT_DIMS, preferred_element_type=f32)
            o_i = o_i + jnp.dot(att.astype(bf16), v[klo:khi], preferred_element_type=f32)
        half = sub // 2
        parts = [o_i[:half], o_i[half:]]
        for s in range(sub):
            for p in range(2):
                t0 = p * half
                if (t0 > s) if rev else (t0 + half - 1 < s):
                    continue
                ok = (t_io[t0:t0 + half] <= s) if rev else (t_io[t0:t0 + half] >= s)
                e = jnp.exp(jnp.where(ok, cum_i[t0:t0 + half] - cum_i[s:s + 1], NEG))
                a_col = jnp.sum(q_i[t0:t0 + half] * k_i[s:s + 1] * e, axis=-1, keepdims=True)
                parts[p] = parts[p] + a_col * v_i[s:s + 1]
        o_i = jnp.concatenate(parts, axis=0)
        outs.append(o_i)
    return o_inter + jnp.concatenate(outs, axis=0)


def _gla_kernel(qf_ref, kf_ref, vf_ref, ddf_ref, qb_ref, kb_ref, vb_ref, ddb_ref, kc_ref, vc_ref, ddc_ref,
                upf_ref, upb_ref, bf_ref, bb_ref, of_ref, ob_ref, sf_scr, sb_scr):
    j = pl.program_id(2)
    n_chunks = kc_ref.shape[0] // GLA_CHUNK
    upf, upb, bias_f, bias_b = upf_ref[0], upb_ref[0], bf_ref[...], bb_ref[...]

    def chunk(ci):
        return slice(ci * GLA_CHUNK, (ci + 1) * GLA_CHUNK)

    @pl.when(j == 0)
    def _():
        sf_scr[...] = jnp.zeros_like(sf_scr)
        sb_scr[...] = jnp.zeros_like(sb_scr)

        cum_f = _chunk_cumsums(ddc_ref[...], upf, bias_f, rev=False)
        cum_b = _chunk_cumsums(ddc_ref[...], upb, bias_b, rev=True)
        for ci in range(n_chunks):
            cf, cb = chunk(ci), chunk(n_chunks - 1 - ci)
            _gla_chunk(None, kc_ref[cf, :].astype(f32), vc_ref[cf, :], cum_f[cf], sf_scr, rev=False,
                       need_out=False)
            _gla_chunk(None, kc_ref[cb, :].astype(f32), vc_ref[cb, :], cum_b[cb], sb_scr, rev=True,
                       need_out=False)

    @pl.when(j > 0)
    def _():
        nc = kf_ref.shape[0] // GLA_CHUNK
        cum_f = _chunk_cumsums(ddf_ref[...], upf, bias_f, rev=False)
        cum_b = _chunk_cumsums(ddb_ref[...], upb, bias_b, rev=True)
        for ci in range(nc):
            cf, cb = chunk(ci), chunk(nc - 1 - ci)
            o_f = _gla_chunk(qf_ref[cf, :].astype(f32), kf_ref[cf, :].astype(f32), vf_ref[cf, :], cum_f[cf], sf_scr,
                             rev=False, need_out=True)
            o_b = _gla_chunk(qb_ref[cb, :].astype(f32), kb_ref[cb, :].astype(f32), vb_ref[cb, :], cum_b[cb], sb_scr,
                             rev=True, need_out=True)
            of_ref[cf, :] = o_f.astype(bf16)
            ob_ref[cb, :] = o_b.astype(bf16)


def _gla(p_lat, dd_lat, p_ctx, dd_ctx, upf, upb, bias_f, bias_b, batch, l, lc):
    tt = lc
    n_l = l // tt
    hq, hk, hv = COL_GQ // GLA_DK, COL_GK // GLA_DK, COL_GV // GLA_DV

    def fwd(b, h, j):
        return b * n_l + jnp.maximum(j - 1, 0)

    def bwd(b, h, j):
        return b * n_l + n_l - jnp.maximum(j, 1)

    lat = lambda rowmap, width, col: pl.BlockSpec((tt, width), lambda b, h, j: (rowmap(b, h, j), col + h))
    ddspec = lambda rowmap: pl.BlockSpec((tt, 128), lambda b, h, j: (rowmap(b, h, j), 0))
    ctx = lambda width, col: pl.BlockSpec((tt, width), lambda b, h, j: (b, col + h))
    return pl.pallas_call(
        _gla_kernel,
        out_shape=[jax.ShapeDtypeStruct((batch * l, GLA_VAL_WIDTH), bf16)] * 2,
        grid=(batch, GLA_HEADS, n_l + 1),
        in_specs=[lat(fwd, GLA_DK, hq), lat(fwd, GLA_DK, hk), lat(fwd, GLA_DV, hv), ddspec(fwd),
                  lat(bwd, GLA_DK, hq), lat(bwd, GLA_DK, hk), lat(bwd, GLA_DV, hv), ddspec(bwd),
                  ctx(GLA_DK, hk), ctx(GLA_DV, hv), pl.BlockSpec((tt, 128), lambda b, h, j: (b, 0)),
                  pl.BlockSpec((1, 128, GLA_DK), lambda b, h, j: (h, 0, 0)),
                  pl.BlockSpec((1, 128, GLA_DK), lambda b, h, j: (h, 0, 0)),
                  pl.BlockSpec((1, GLA_DK), lambda b, h, j: (0, h)),
                  pl.BlockSpec((1, GLA_DK), lambda b, h, j: (0, h))],
        out_specs=[pl.BlockSpec((tt, GLA_DV), lambda b, h, j: (fwd(b, h, j), h)),
                   pl.BlockSpec((tt, GLA_DV), lambda b, h, j: (bwd(b, h, j), h))],
        scratch_shapes=[pltpu.VMEM((GLA_DV, GLA_DK), f32), pltpu.VMEM((GLA_DV, GLA_DK), f32)],
        compiler_params=_cparams(("arbitrary", "arbitrary", "arbitrary")),
        name="gla",
    )(p_lat, p_lat, p_lat, dd_lat, p_lat, p_lat, p_lat, dd_lat, p_ctx, p_ctx, dd_ctx, upf, upb, bias_f, bias_b)


def _pad_gate_up(up, row0):
    w = up.reshape(GLA_GATE_RANK, GLA_HEADS, GLA_DK).transpose(1, 0, 2)
    return jnp.zeros((GLA_HEADS, 128, GLA_DK), f32).at[:, row0:row0 + GLA_GATE_RANK].set(w).astype(bf16)


def _outproj_kernel(ona_ref, of_ref, ob_ref, gate_ref, gng_ref, w_ref, x_ref, g1_ref, n2g_ref, sc2_ref, sh2_ref,
                    rwt_ref, g2_ref, ws1_ref, ws3_ref, ws2_ref, x1s_ref, h2_ref, lg_ref):
    o = of_ref[...].astype(f32) + ob_ref[...].astype(f32)
    gla = (_head_rms(o, gng_ref[...], GLA_DV) * gate_ref[...].astype(f32)).astype(bf16)
    acc = (jnp.dot(ona_ref[...], w_ref[0:NA_WIDTH, :], preferred_element_type=f32)
           + jnp.dot(gla, w_ref[NA_WIDTH:, :], preferred_element_type=f32))
    x1 = x_ref[...] + g1_ref[0] * acc
    ms = jnp.mean(x1 * x1, axis=-1, keepdims=True)
    h2 = x1 * lax.rsqrt(ms + NORM_EPS) * n2g_ref[...] * (1.0 + sc2_ref[0]) + sh2_ref[0]
    h2_ref[:, 0, :] = h2
    hb = h2.astype(bf16)
    h_lo = (h2 - hb.astype(f32)).astype(bf16)
    rw = rwt_ref[...]
    rw_hi = rw.astype(bf16)
    rw_lo = (rw - rw_hi.astype(f32)).astype(bf16)
    lg_ref[...] = (lax.dot_general(rw_hi, hb, NT_DIMS, preferred_element_type=f32)
                   + lax.dot_general(rw_hi, h_lo, NT_DIMS, preferred_element_type=f32)
                   + lax.dot_general(rw_lo, hb, NT_DIMS, preferred_element_type=f32))
    s1 = jnp.dot(hb, ws1_ref[...], preferred_element_type=f32)
    s3 = jnp.dot(hb, ws3_ref[...], preferred_element_type=f32)
    act = (s1 * jax.nn.sigmoid(s1) * s3).astype(bf16)
    x1s_ref[...] = x1 + g2_ref[0] * jnp.dot(act, ws2_ref[...], preferred_element_type=f32)


def _outproj(o_na, o_f, o_b, p_lat, gnorm_g, w_out_bf, x2d, g1, norm2_g, sc2, sh2, rwt, g2, ws1, ws3, ws2,
             tiles_per_batch, tm):
    m, d = x2d.shape
    vec = lambda i: (0, 0)
    per_b = lambda i: (i // tiles_per_batch, 0, 0)
    return pl.pallas_call(
        _outproj_kernel,
        out_shape=[jax.ShapeDtypeStruct((m, d), f32), jax.ShapeDtypeStruct((m, 1, d), f32),
                   jax.ShapeDtypeStruct((N_EXPERTS, m), f32)],
        grid=(m // tm,),
        in_specs=[pl.BlockSpec((tm, NA_WIDTH), lambda i: (i, 0)),
                  pl.BlockSpec((tm, GLA_VAL_WIDTH), lambda i: (i, 0)),
                  pl.BlockSpec((tm, GLA_VAL_WIDTH), lambda i: (i, 0)),
                  pl.BlockSpec((tm, GLA_VAL_WIDTH), lambda i: (i, COL_GATE // GLA_VAL_WIDTH)),
                  pl.BlockSpec((1, GLA_DV), vec),
                  pl.BlockSpec(w_out_bf.shape, vec),
                  pl.BlockSpec((tm, d), lambda i: (i, 0)),
                  pl.BlockSpec((1, 1, d), per_b),
                  pl.BlockSpec((1, d), vec),
                  pl.BlockSpec((1, 1, d), per_b),
                  pl.BlockSpec((1, 1, d), per_b),
                  pl.BlockSpec((N_EXPERTS, d), vec),
                  pl.BlockSpec((1, 1, d), per_b),
                  pl.BlockSpec(ws1.shape, vec),
                  pl.BlockSpec(ws3.shape, vec),
                  pl.BlockSpec(ws2.shape, vec)],
        out_specs=[pl.BlockSpec((tm, d), lambda i: (i, 0)),
                   pl.BlockSpec((tm, 1, d), lambda i: (i, 0, 0)),
                   pl.BlockSpec((N_EXPERTS, tm), lambda i: (0, i))],
        compiler_params=_cparams(("arbitrary",)),
        name="outproj",
    )(o_na, o_f, o_b, p_lat, gnorm_g, w_out_bf, x2d, g1, norm2_g, sc2, sh2, rwt, g2, ws1, ws3, ws2)


def _first_argmax(vals, iota, axis, sentinel):
    mx = jnp.max(vals, axis=axis, keepdims=True)
    am = jnp.min(jnp.where(vals == mx, iota, sentinel), axis=axis, keepdims=True)
    return mx, am


def _route_tile(lg, rb):
    t = lg.shape[1]
    per = N_EXPERTS // N_GROUPS
    scores = jax.nn.sigmoid(lg)
    biased = scores + rb
    b3 = biased.reshape(N_GROUPS, per, t)
    io3 = lax.broadcasted_iota(i32, (N_GROUPS, per, t), 1)
    m1, a1 = _first_argmax(b3, io3, 1, per)
    m2 = jnp.max(jnp.where(io3 == a1, -jnp.inf, b3), axis=1, keepdims=True)
    gs = (m1 + m2).reshape(N_GROUPS, t)
    iog = lax.broadcasted_iota(i32, (N_GROUPS, t), 0)
    gsel = jnp.zeros((N_GROUPS, t), jnp.bool_)
    for _ in range(TOPK_GROUPS):
        _, am = _first_argmax(gs, iog, 0, N_GROUPS)
        hit = iog == am
        gsel = gsel | hit
        gs = jnp.where(hit, -jnp.inf, gs)
    emask = jnp.broadcast_to(gsel[:, None, :], (N_GROUPS, per, t)).reshape(N_EXPERTS, t)
    masked = jnp.where(emask, biased, -jnp.inf)
    ioe = lax.broadcasted_iota(i32, (N_EXPERTS, t), 0)
    hits, sels = [], []
    for _ in range(TOP_K):
        _, am = _first_argmax(masked, ioe, 0, N_EXPERTS)
        hit = ioe == am
        hits.append(hit)
        sels.append(jnp.sum(jnp.where(hit, scores, 0.0), axis=0, keepdims=True))
        masked = jnp.where(hit, -jnp.inf, masked)
    denom = sels[0]
    for s in sels[1:]:
        denom = denom + s
    wts = [s / denom * ROUTED_SCALE for s in sels]
    return hits, wts


def _route_kernel(lg_ref, rb_ref, pos_ref, wts_ref, meta_ref, blk_ref, cnt_scr, carry_scr, base_scr, tri_scr):
    ph = pl.program_id(0)
    i = pl.program_id(1)
    t = lg_ref.shape[1]
    hits, wts = _route_tile(lg_ref[...], rb_ref[...])
    sel = hits[0]
    for h in hits[1:]:
        sel = sel | h
    self32 = sel.astype(f32)
    tile_cnt = jnp.sum(self32, axis=1, keepdims=True)

    @pl.when((ph == 0) & (i == 0))
    def _():
        cnt_scr[...] = jnp.zeros_like(cnt_scr)

    @pl.when(ph == 0)
    def _():
        cnt_scr[...] += jnp.broadcast_to(tile_cnt, cnt_scr.shape)

    @pl.when((ph == 1) & (i == 0))
    def _():
        cnt = cnt_scr[...]
        nblk = jnp.floor((cnt + (MOE_TM - 1)) * (1.0 / MOE_TM))
        r = lax.broadcasted_iota(i32, (N_EXPERTS, N_EXPERTS), 0)
        c = lax.broadcasted_iota(i32, (N_EXPERTS, N_EXPERTS), 1)
        excl = jnp.dot((c < r).astype(bf16), nblk.astype(bf16), preferred_element_type=f32)
        base_scr[...] = excl * MOE_TM
        carry_scr[...] = jnp.zeros_like(carry_scr)
        meta_ref[0] = cnt.astype(i32)
        meta_ref[1] = (excl * MOE_TM).astype(i32)
        incl = excl + nblk
        nb = blk_ref.shape[1]
        jj = lax.broadcasted_iota(i32, (N_EXPERTS, nb), 1).astype(f32)
        owner = jnp.sum((jnp.broadcast_to(incl[:, 0:1], (N_EXPERTS, nb)) <= jj).astype(i32), axis=0, keepdims=True)
        blk_ref[0:1, :] = jnp.minimum(owner, N_EXPERTS - 1)
        blk_ref[1:2, :] = jnp.broadcast_to(incl[N_EXPERTS - 1:N_EXPERTS, 0:1].astype(i32), (1, nb))
        blk_ref[2:8, :] = jnp.zeros((6, nb), i32)
        rr = lax.broadcasted_iota(i32, (t, t), 0)
        cc = lax.broadcasted_iota(i32, (t, t), 1)
        tri_scr[...] = (rr < cc).astype(bf16)

    @pl.when(ph == 1)
    def _():
        rank = jnp.dot(sel.astype(bf16), tri_scr[...], preferred_element_type=f32)
        posf = base_scr[:, 0:1] + carry_scr[:, 0:1] + rank
        for k in range(TOP_K):
            pos_ref[k:k + 1, :] = jnp.sum(jnp.where(hits[k], posf, 0.0), axis=0, keepdims=True).astype(i32)
            wts_ref[k:k + 1, :] = wts[k]
        pos_ref[TOP_K:8, :] = jnp.zeros((8 - TOP_K, t), i32)
        wts_ref[TOP_K:8, :] = jnp.zeros((8 - TOP_K, t), f32)
        carry_scr[...] += jnp.broadcast_to(tile_cnt, carry_scr.shape)


def _route(logits_t, router_bias, n_blocks):
    n_tok = logits_t.shape[1]
    tt = 1024
    n_tiles = n_tok // tt
    return pl.pallas_call(
        _route_kernel,
        out_shape=[jax.ShapeDtypeStruct((8, n_tok), i32), jax.ShapeDtypeStruct((8, n_tok), f32),
                   jax.ShapeDtypeStruct((2, N_EXPERTS, 128), i32), jax.ShapeDtypeStruct((8, n_blocks), i32)],
        grid=(2, n_tiles),
        in_specs=[pl.BlockSpec((N_EXPERTS, tt), lambda p, i: (0, i)),
                  pl.BlockSpec((N_EXPERTS, 1), lambda p, i: (0, 0))],
        out_specs=[pl.BlockSpec((8, tt), lambda p, i: (0, i * p)),
                   pl.BlockSpec((8, tt), lambda p, i: (0, i * p)),
                   pl.BlockSpec((2, N_EXPERTS, 128), lambda p, i: (0, 0, 0)),
                   pl.BlockSpec((8, n_blocks), lambda p, i: (0, 0))],
        scratch_shapes=[pltpu.VMEM((N_EXPERTS, 128), f32), pltpu.VMEM((N_EXPERTS, 128), f32),
                        pltpu.VMEM((N_EXPERTS, 128), f32), pltpu.VMEM((tt, tt), bf16)],
        compiler_params=_cparams(("arbitrary", "arbitrary")),
        name="route",
    )(logits_t, router_bias.reshape(N_EXPERTS, 1))


def _experts_kernel(blk_e_ref, nblk_ref, pos_ref, cnt_ref, base_ref, h2_ref, w1_hbm, w3_hbm, w2_hbm, ys_ref,
                    tok_smem, ord_smem, xbuf, sems, wbuf1, wbuf3, wbuf2, wsems, w1_scr, w3_scr, w2_scr):
    j = pl.program_id(0)
    n_valid = nblk_ref[0]
    n_tok = pos_ref.shape[0] // TOP_K
    n_slots = GATHER_AHEAD + 1

    def row_copy(row, slot, r):
        return pltpu.make_async_copy(h2_ref.at[tok_smem[row]], xbuf.at[slot, pl.ds(r, 1)], sems.at[slot])

    def weight_copies(expert, slot):
        return [pltpu.make_async_copy(hbm.at[expert], buf.at[slot], wsems.at[slot])
                for hbm, buf in ((w1_hbm, wbuf1), (w3_hbm, wbuf3), (w2_hbm, wbuf2))]

    def start_gather(block, slot):
        def issue(r, carry):
            row_copy(block * MOE_TM + r, slot, r).start()
            return carry

        lax.fori_loop(0, MOE_TM, issue, 0, unroll=8)

    def drain_gather(slot):
        def drain(r, carry):
            row_copy(0, slot, 0).wait()
            return carry

        lax.fori_loop(0, MOE_TM, drain, 0, unroll=8)

    @pl.when(j == 0)
    def _():
        ord_smem[0] = 0
        for c in weight_copies(blk_e_ref[0], 0):
            c.start(priority=1)

        def pad_expert(e, carry):
            cnt = cnt_ref[e]
            first = base_ref[e] + cnt

            def fill(r, c):
                tok_smem[first + r] = 0
                return c

            lax.fori_loop(0, (MOE_TM - cnt % MOE_TM) % MOE_TM, fill, 0)
            return carry

        lax.fori_loop(0, N_EXPERTS, pad_expert, 0)

        def invert(t, carry):
            for k in range(TOP_K):
                tok_smem[pos_ref[t * TOP_K + k]] = t
            return carry

        lax.fori_loop(0, n_tok, invert, 0, unroll=8)
        for ahead in range(GATHER_AHEAD):
            start_gather(jnp.minimum(ahead, n_valid - 1), ahead)

    e = blk_e_ref[jnp.minimum(j, n_valid - 1)]
    e_prev = blk_e_ref[jnp.maximum(j - 1, 0)]

    @pl.when((j == 0) | ((j < n_valid) & (e != e_prev)))
    def _():
        ordinal = ord_smem[0]
        slot = ordinal % 2
        for c in weight_copies(e, slot):
            c.wait()
        next_blk = (base_ref[e] + cnt_ref[e] + (MOE_TM - 1)) // MOE_TM

        @pl.when(next_blk < n_valid)
        def _():
            for c in weight_copies(blk_e_ref[jnp.minimum(next_blk, n_valid - 1)], 1 - slot):
                c.start(priority=1)

        w1_scr[...] = wbuf1[slot].astype(bf16)
        w3_scr[...] = wbuf3[slot].astype(bf16)
        w2_scr[...] = wbuf2[slot].astype(bf16)
        ord_smem[0] = ordinal + 1

    @pl.when(j < n_valid)
    def _():
        slot = j % n_slots
        drain_gather(slot)
        x = xbuf[slot].astype(bf16)
        nxt = jnp.minimum(j + GATHER_AHEAD, n_valid - 1) * MOE_TM
        nxt_slot = (j + GATHER_AHEAD) % n_slots
        for r in range(MOE_TM):
            row_copy(nxt + r, nxt_slot, r).start(priority=0)
        y = None
        for c0 in range(0, w1_scr.shape[1], EXPERT_HIDDEN_TN):
            cols = slice(c0, c0 + EXPERT_HIDDEN_TN)
            h1 = jnp.dot(x, w1_scr[:, cols], preferred_element_type=f32)
            h3 = jnp.dot(x, w3_scr[:, cols], preferred_element_type=f32)
            act = (h1 * jax.nn.sigmoid(h1) * h3).astype(bf16)
            part = jnp.dot(act, w2_scr[cols, :], preferred_element_type=f32)
            y = part if y is None else y + part
        ys_ref[:, 0, :] = y

    @pl.when(j == n_valid - 1)
    def _():
        for ahead in range(1, GATHER_AHEAD + 1):
            drain_gather((j + ahead) % n_slots)

    @pl.when(j >= n_valid)
    def _():
        ys_ref[...] = jnp.zeros_like(ys_ref)


def _experts(blk_expert, nblk, pos_flat, cnt, base, h2, w1, w3, w2, n_blocks):
    d, de = w1.shape[1], w1.shape[2]
    n_rows = n_blocks * MOE_TM

    hbm = pl.BlockSpec(memory_space=pl.ANY)
    return pl.pallas_call(
        _experts_kernel,
        out_shape=jax.ShapeDtypeStruct((n_rows, 1, d), f32),
        grid_spec=pltpu.PrefetchScalarGridSpec(
            num_scalar_prefetch=5,
            grid=(n_blocks,),
            in_specs=[hbm, hbm, hbm, hbm],
            out_specs=pl.BlockSpec((MOE_TM, 1, d), lambda j, *_: (j, 0, 0)),
            scratch_shapes=[pltpu.SMEM((n_rows,), i32), pltpu.SMEM((1,), i32),
                            pltpu.VMEM((GATHER_AHEAD + 1, MOE_TM, d), f32),
                            pltpu.SemaphoreType.DMA((GATHER_AHEAD + 1,)),
                            pltpu.VMEM((2, d, de), f32), pltpu.VMEM((2, d, de), f32), pltpu.VMEM((2, de, d), f32),
                            pltpu.SemaphoreType.DMA((2,)),
                            pltpu.VMEM((d, de), bf16), pltpu.VMEM((d, de), bf16), pltpu.VMEM((de, d), bf16)]),
        compiler_params=_cparams(("arbitrary",)),
        name="experts",
    )(blk_expert, nblk, pos_flat, cnt, base, h2, w1, w3, w2)


def _combine_kernel(pos_ref, ys_ref, wts_ref, x1s_ref, g2_ref, o_ref, ybuf, sems):
    i = pl.program_id(0)
    n_tiles = 2 * pl.num_programs(0)
    tm = COMBINE_TM

    def row_copy(src_row, slot, k, t):
        return pltpu.make_async_copy(ys_ref.at[src_row], ybuf.at[slot, k, pl.ds(t, 1)], sems.at[slot])

    def drain_gather(slot):
        def drain(t, carry):
            for k in range(TOP_K):
                row_copy(0, slot, k, 0).wait()
            return carry

        lax.fori_loop(0, tm, drain, 0, unroll=2)

    def half_step(slot, next_tile):
        drain_gather(slot)
        for t in range(tm):
            for k in range(TOP_K):
                row_copy(pos_ref[(next_tile * tm + t) * TOP_K + k], 1 - slot, k, t).start(priority=(t + k) % 2)
        rows = slice(slot * tm, (slot + 1) * tm)
        w = wts_ref[rows, :]
        acc = w[:, 0:1] * ybuf[slot, 0]
        for k in range(1, TOP_K):
            acc = acc + w[:, k:k + 1] * ybuf[slot, k]
        o_ref[rows, :] = x1s_ref[rows, :] + g2_ref[0] * acc

    @pl.when(i == 0)
    def _():
        def issue(t, carry):
            for k in range(TOP_K):
                row_copy(pos_ref[t * TOP_K + k], 0, k, t).start()
            return carry

        lax.fori_loop(0, tm, issue, 0, unroll=2)

    half_step(0, 2 * i + 1)
    half_step(1, jnp.minimum(2 * i + 2, n_tiles - 1))

    @pl.when(i == pl.num_programs(0) - 1)
    def _():
        drain_gather(0)


def _combine(pos_flat, ys, wts_t, x1s, g2, tokens_per_batch):
    n_tok, d = x1s.shape
    tm = 2 * COMBINE_TM
    return pl.pallas_call(
        _combine_kernel,
        out_shape=jax.ShapeDtypeStruct((n_tok, d), f32),
        grid=(n_tok // tm,),
        in_specs=[pl.BlockSpec(memory_space=pltpu.SMEM),
                  pl.BlockSpec(memory_space=pl.ANY),
                  pl.BlockSpec((tm, 8), lambda i: (i, 0)),
                  pl.BlockSpec((tm, d), lambda i: (i, 0)),
                  pl.BlockSpec((1, 1, d), lambda i: (i // (tokens_per_batch // tm), 0, 0))],
        out_specs=pl.BlockSpec((tm, d), lambda i: (i, 0)),
        scratch_shapes=[pltpu.VMEM((2, TOP_K, COMBINE_TM, d), f32), pltpu.SemaphoreType.DMA((2,))],
        compiler_params=_cparams(("arbitrary",)),
        name="combine",
    )(pos_flat, ys, wts_t, x1s, g2)


def _layer(x, ctx, mod, norm1_g, norm2_g, w_in, q_norm_g, k_norm_g, na_rpb, up_f, bias_f, up_b, bias_b, gla_norm_g,
           w_out, router_w, router_bias, w1, w3, w2, ws1, ws3, ws2):
    batch, l, d = x.shape
    lc = ctx.shape[1]
    n_tok = batch * l
    sh1, sc1, g1, sh2, sc2, g2 = [mod[:batch, k * d:(k + 1) * d].reshape(batch, 1, d) for k in range(6)]
    csh1, csc1 = [mod[batch:batch + 1, k * d:(k + 1) * d].reshape(1, 1, d) for k in range(2)]

    w_bf = w_in.astype(bf16)
    wdd_bf = jnp.zeros((d, 128), f32).at[:, :2 * GLA_GATE_RANK].set(w_in[:, COL_DD:]).astype(bf16)
    qg, kg = q_norm_g.reshape(1, -1), k_norm_g.reshape(1, -1)
    tabs = _rope_tables(l)
    n1 = norm1_g.reshape(1, d)
    tm_lat = min(INPROJ_TM, l)
    p_lat, dd_lat = _inproj(x.reshape(n_tok, d), sc1, sh1, n1, w_bf, wdd_bf, qg, kg, tabs, tm=tm_lat,
                            tiles_per_group=l // tm_lat, rope=True)
    tm_ctx = min(INPROJ_TM, batch * lc)
    ctabs = tuple(t[:tm_ctx] for t in tabs)
    p_ctx, dd_ctx = _inproj(ctx.reshape(batch * lc, d), csc1, csh1, n1, w_bf, wdd_bf, qg, kg, ctabs, tm=tm_ctx,
                            tiles_per_group=batch * lc // tm_ctx, rope=False)

    bias = _na_bias_tables(na_rpb, l // GRID_W)
    o_na = _na(p_lat, p_ctx, bias, batch, l, lc)

    o_f, o_b = _gla(p_lat, dd_lat, p_ctx, dd_ctx, _pad_gate_up(up_f, 0), _pad_gate_up(up_b, GLA_GATE_RANK),
                    bias_f.reshape(1, -1), bias_b.reshape(1, -1), batch, l, lc)

    tm_out = 256
    x1s, h2, logits_t = _outproj(o_na, o_f, o_b, p_lat, gla_norm_g.reshape(1, -1), w_out.astype(bf16),
                                 x.reshape(n_tok, d), g1, norm2_g.reshape(1, d), sc2, sh2, router_w.T, g2,
                                 ws1.astype(bf16), ws3.astype(bf16), ws2.astype(bf16), l // tm_out, tm_out)

    n_blocks = n_tok * TOP_K // MOE_TM + N_EXPERTS
    pos, wts, meta, blk = _route(logits_t, router_bias, n_blocks)
    pos_flat = pos[:TOP_K].T.reshape(-1)
    ys = _experts(blk[0], blk[1, :1], pos_flat, meta[0, :, 0], meta[1, :, 0], h2, w1, w3, w2, n_blocks)
    out = _combine(pos_flat, ys, wts.T, x1s, g2, l)
    return out.reshape(batch, l, d)


def kernel(x, c, ctx, c_ctx, w_mod, b_mod, norm1_g, norm2_g, w_in, q_norm_g, k_norm_g, na_rpb, gla_gate_up_f,
           gla_gate_bias_f, gla_gate_up_b, gla_gate_bias_b, gla_norm_g, w_out, router_w, router_bias, expert_w1,
           expert_w3, expert_w2, shared_w1, shared_w3, shared_w2):
    depth = w_mod.shape[0]
    assert depth == 1, "context-stream outputs are only dropped for a single (last) layer"
    batch, d = c.shape
    cond8 = jnp.zeros((8, d), f32).at[:batch].set(c).at[batch].set(c_ctx)
    mod = _adaln(cond8, w_mod[0], b_mod[0])
    return _layer(x, ctx, mod, norm1_g[0], norm2_g[0], w_in[0], q_norm_g[0], k_norm_g[0], na_rpb[0],
                  gla_gate_up_f[0], gla_gate_bias_f[0], gla_gate_up_b[0], gla_gate_bias_b[0], gla_norm_g[0],
                  w_out[0], router_w[0], router_bias[0], expert_w1[0], expert_w3[0], expert_w2[0], shared_w1[0],
                  shared_w3[0], shared_w2[0])
```
